```python
import jax, jax.numpy as jnp
from jax import lax
import numpy as np

D_MODEL = 1024
BATCH = 8
SEQ = 2048
DEPTH = 1
DEC_BATCH = 4
DEC_SEQ = 4096
PAST_LEN = 128

D_MIX = D_MODEL
D_GMLP = D_MIX // 2
D_HGRN = D_MIX - D_GMLP
GMLP_HEADS = 4
GMLP_HEAD_DIM = D_GMLP // GMLP_HEADS
GMLP_CHUNK = 128
HGRN_HEAD_DIM = 128
HGRN_HEADS = D_HGRN // HGRN_HEAD_DIM
HGRN_CHUNK = 16
D_FF = 2816
EPS = 1e-6
D_IN = 2 * D_GMLP + 5 * D_HGRN
IN_SPLIT = (D_GMLP, 2 * D_GMLP, 2 * D_GMLP + D_HGRN, 2 * D_GMLP + 2 * D_HGRN,
            2 * D_GMLP + 3 * D_HGRN, 2 * D_GMLP + 4 * D_HGRN)

kernel_name = "hybrid_gmlp_hgrn2_macaron_encoder"


def rms_norm(x, gain):
    xf = x.astype(jnp.float32)
    y = xf * lax.rsqrt(jnp.mean(xf * xf, axis=-1, keepdims=True) + EPS)
    return (y * gain.astype(jnp.float32)).astype(x.dtype)


def layer_norm(x, gain, bias):
    xf = x.astype(jnp.float32)
    mu = jnp.mean(xf, axis=-1, keepdims=True)
    xc = xf - mu
    y = xc * lax.rsqrt(jnp.mean(xc * xc, axis=-1, keepdims=True) + EPS)
    return (y * gain.astype(jnp.float32) + bias.astype(jnp.float32)).astype(x.dtype)


def swiglu(x, w_gate, w_up, w_down):
    return (jax.nn.silu(x @ w_gate) * (x @ w_up)) @ w_down


def gmlp_mixer(u, v, ln_g, ln_b, w_s, b_s):
    B, L, _ = v.shape
    v = layer_norm(v, ln_g, ln_b)
    vc = v.reshape(B, L // GMLP_CHUNK, GMLP_CHUNK, GMLP_HEADS, GMLP_HEAD_DIM)
    mixed = jnp.einsum('hts,bcshd->bcthd', w_s, vc) + b_s.T[None, None, :, :, None]
    return u * mixed.reshape(B, L, D_GMLP)


def hgrn2_chunkwise(q, k, v, log_f):
    B, L, H, DK = q.shape
    DV = v.shape[-1]
    C = HGRN_CHUNK
    n = L // C
    q, k, v, log_f = (t.reshape(B, n, C, H, t.shape[-1]) for t in (q, k, v, log_f))
    a = jnp.cumsum(log_f, axis=2)
    a_last = a[:, :, -1]
    lower = jnp.tril(jnp.ones((C, C), dtype=bool))[None, None, :, :, None, None]
    diff = a[:, :, :, None] - a[:, :, None, :]
    decay = jnp.exp(jnp.where(lower, diff, -jnp.inf))
    scores = jnp.einsum('bnthk,bnshk,bntshk->bnhts', q, k, decay)
    o_intra = jnp.einsum('bnhts,bnshv->bnthv', scores, v)
    chunk_kv = jnp.einsum('bnshk,bnshv->bnhkv', k * jnp.exp(a_last[:, :, None] - a), v)
    chunk_decay = jnp.exp(a_last)

    def step(S, inp):
        dec, kv = inp
        return dec[..., None] * S + kv, S

    S0 = jnp.zeros((B, H, DK, DV), jnp.float32)
    _, S_start = lax.scan(step, S0, (jnp.moveaxis(chunk_decay, 1, 0), jnp.moveaxis(chunk_kv, 1, 0)))
    S_start = jnp.moveaxis(S_start, 0, 1)
    o_inter = jnp.einsum('bnthk,bnhkv->bnthv', q * jnp.exp(a), S_start)
    return (o_intra + o_inter).reshape(B, L, H, DV)


def hgrn2_mixer(q_raw, i_raw, f_fwd_raw, f_bwd_raw, g_raw, lb_fwd, lb_bwd, out_norm):
    B, L, _ = q_raw.shape
    heads = lambda t: t.astype(jnp.float32).reshape(B, L, HGRN_HEADS, HGRN_HEAD_DIM)
    q = heads(jax.nn.silu(q_raw))
    v = heads(i_raw)

    def direction(f_raw, lb, reverse):
        lbh = lb.reshape(HGRN_HEADS, HGRN_HEAD_DIM)
        f = lbh + (1.0 - lbh) * jax.nn.sigmoid(heads(f_raw))
        k = 1.0 - f
        log_f = jnp.log(f)
        if reverse:
            qd, kd, vd, lfd = (jnp.flip(t, axis=1) for t in (q, k, v, log_f))
            return jnp.flip(hgrn2_chunkwise(qd, kd, vd, lfd), axis=1)
        return hgrn2_chunkwise(q, k, v, log_f)

    o = direction(f_fwd_raw, lb_fwd, False) + direction(f_bwd_raw, lb_bwd, True)
    o = rms_norm(o, out_norm.reshape(HGRN_HEADS, HGRN_HEAD_DIM)).reshape(B, L, D_HGRN)
    return (o * jax.nn.silu(g_raw.astype(jnp.float32))).astype(q_raw.dtype)


def trunk(x, ffn1_norm, ffn1_w_gate, ffn1_w_up, ffn1_w_down, mix_norm, w_in,
          gmlp_ln_g, gmlp_ln_b, gmlp_w_s, gmlp_b_s, gmlp_out_norm,
          hgrn_lb_fwd, hgrn_lb_bwd, hgrn_out_norm, w_out,
          ffn2_norm, ffn2_w_gate, ffn2_w_up, ffn2_w_down, final_norm):
    lb_fwd_all = jnp.cumsum(jax.nn.softmax(hgrn_lb_fwd.astype(jnp.float32), axis=0), axis=0)
    lb_bwd_all = jnp.cumsum(jax.nn.softmax(hgrn_lb_bwd.astype(jnp.float32), axis=0), axis=0)
    for l in range(DEPTH):
        h = rms_norm(x, ffn1_norm[l])
        x = x + 0.5 * swiglu(h, ffn1_w_gate[l], ffn1_w_up[l], ffn1_w_down[l])
        h = rms_norm(x, mix_norm[l])
        proj = h @ w_in[l]
        u, v, q, i, f_f, f_b, g = jnp.split(proj, IN_SPLIT, axis=-1)
        y_g = gmlp_mixer(jax.nn.gelu(u, approximate=False), jax.nn.gelu(v, approximate=False),
                         gmlp_ln_g[l], gmlp_ln_b[l], gmlp_w_s[l], gmlp_b_s[l])
        y_g = rms_norm(y_g, gmlp_out_norm[l])
        y_h = hgrn2_mixer(q, i, f_f, f_b, g, lb_fwd_all[l].astype(x.dtype), lb_bwd_all[l].astype(x.dtype),
                          hgrn_out_norm[l])
        x = x + jnp.concatenate([y_g, y_h], axis=-1) @ w_out[l]
        h = rms_norm(x, ffn2_norm[l])
        x = x + 0.5 * swiglu(h, ffn2_w_gate[l], ffn2_w_up[l], ffn2_w_down[l])
    return rms_norm(x, final_norm)


def setup_inputs(seed: int = 0) -> dict:
    key = jax.random.key(seed)
    ks = jax.random.split(key, 24)
    nrm = lambda k, shape, scale: jax.random.normal(k, shape, jnp.float32) * scale
    gain = lambda k, shape: 1.0 + 0.02 * jax.random.normal(k, shape, jnp.float32)
    return {
        "x_prompt": nrm(ks[0], (BATCH, SEQ, D_MODEL), 1.0),
        "x_sample": nrm(ks[1], (DEC_BATCH, DEC_SEQ, D_MODEL), 1.0),
        "ffn1_norm": gain(ks[2], (DEPTH, D_MODEL)),
        "ffn1_w_gate": nrm(ks[3], (DEPTH, D_MODEL, D_FF), D_MODEL ** -0.5),
        "ffn1_w_up": nrm(ks[4], (DEPTH, D_MODEL, D_FF), D_MODEL ** -0.5),
        "ffn1_w_down": nrm(ks[5], (DEPTH, D_FF, D_MODEL), D_FF ** -0.5),
        "mix_norm": gain(ks[6], (DEPTH, D_MODEL)),
        "w_in": nrm(ks[7], (DEPTH, D_MODEL, D_IN), D_MODEL ** -0.5),
        "gmlp_ln_g": gain(ks[8], (DEPTH, D_GMLP)),
        "gmlp_ln_b": nrm(ks[9], (DEPTH, D_GMLP), 0.02),
        "gmlp_w_s": nrm(ks[10], (DEPTH, GMLP_HEADS, GMLP_CHUNK, GMLP_CHUNK), GMLP_CHUNK ** -0.5),
        "gmlp_b_s": gain(ks[11], (DEPTH, GMLP_HEADS, GMLP_CHUNK)),
        "gmlp_out_norm": gain(ks[12], (DEPTH, D_GMLP)),
        "hgrn_lb_fwd": nrm(ks[13], (DEPTH + 1, D_HGRN), 0.1),
        "hgrn_lb_bwd": nrm(ks[14], (DEPTH + 1, D_HGRN), 0.1),
        "hgrn_out_norm": gain(ks[15], (DEPTH, D_HGRN)),
        "w_out": nrm(ks[16], (DEPTH, D_MIX, D_MODEL), D_MIX ** -0.5),
        "ffn2_norm": gain(ks[17], (DEPTH, D_MODEL)),
        "ffn2_w_gate": nrm(ks[18], (DEPTH, D_MODEL, D_FF), D_MODEL ** -0.5),
        "ffn2_w_up": nrm(ks[19], (DEPTH, D_MODEL, D_FF), D_MODEL ** -0.5),
        "ffn2_w_down": nrm(ks[20], (DEPTH, D_FF, D_MODEL), D_FF ** -0.5),
        "final_norm": gain(ks[21], (D_MODEL,)),
    }


def reference(x_prompt, x_sample, ffn1_norm, ffn1_w_gate, ffn1_w_up, ffn1_w_down, mix_norm, w_in,
              gmlp_ln_g, gmlp_ln_b, gmlp_w_s, gmlp_b_s, gmlp_out_norm,
              hgrn_lb_fwd, hgrn_lb_bwd, hgrn_out_norm, w_out,
              ffn2_norm, ffn2_w_gate, ffn2_w_up, ffn2_w_down, final_norm):
    y_prompt = trunk(x_prompt, ffn1_norm, ffn1_w_gate, ffn1_w_up, ffn1_w_down, mix_norm, w_in,
                     gmlp_ln_g, gmlp_ln_b, gmlp_w_s, gmlp_b_s, gmlp_out_norm,
                     hgrn_lb_fwd, hgrn_lb_bwd, hgrn_out_norm, w_out,
                     ffn2_norm, ffn2_w_gate, ffn2_w_up, ffn2_w_down, final_norm)
    y_sample = trunk(x_sample, ffn1_norm, ffn1_w_gate, ffn1_w_up, ffn1_w_down, mix_norm, w_in,
                     gmlp_ln_g, gmlp_ln_b, gmlp_w_s, gmlp_b_s, gmlp_out_norm,
                     hgrn_lb_fwd, hgrn_lb_bwd, hgrn_out_norm, w_out,
                     ffn2_norm, ffn2_w_gate, ffn2_w_up, ffn2_w_down, final_norm)
    return (y_prompt, y_sample)
```

```python
import functools

import jax
import jax.numpy as jnp
import numpy as np
from jax import lax
from jax.experimental import pallas as pl
from jax.experimental.pallas import tpu as pltpu

F32 = jnp.float32
BF16 = jnp.bfloat16

D_MODEL = 1024
D_GMLP = 512
D_HGRN = 512
N_HEADS = 4
HEAD = 128
CHUNK = 128
HALF = CHUNK // 2
D_FF = 2816
D_IN = 2 * D_GMLP + 5 * D_HGRN
EPS = 1e-6

FF_COLS = 256
FF_STEPS = D_FF // FF_COLS
TOK_PRE = 256
TOK_POST = 256
TOK_MIX = 512
MIX_CHUNKS = TOK_MIX // CHUNK
VMEM_LIMIT = 56 * 1024 * 1024

SQRT_HALF = float(np.sqrt(0.5).astype(np.float32))

NT_DIMS = (((1,), (1,)), ((), ()))
TN_DIMS = (((0,), (0,)), ((), ()))


def _rms(x, gain):
    return x * lax.rsqrt(jnp.mean(x * x, axis=-1, keepdims=True) + EPS) * gain


def _silu(x):
    return x * jax.nn.sigmoid(x)


def _gelu(x):
    return 0.5 * x * (1.0 + lax.erf(x * SQRT_HALF))


def _swiglu_into(acc_ref, hb_ref, wgu_ref, wd_ref):
    acc_ref[...] = jnp.zeros_like(acc_ref)

    def step(c, carry):
        gu = jnp.dot(hb_ref[...], wgu_ref[c], preferred_element_type=F32)
        a = (_silu(gu[:, :FF_COLS]) * gu[:, FF_COLS:]).astype(BF16)
        acc_ref[...] += jnp.dot(a, wd_ref[c], preferred_element_type=F32)
        return carry

    lax.fori_loop(0, FF_STEPS, step, 0)


def _lower_bound(p):
    e = jnp.exp(p - jnp.max(p, axis=0, keepdims=True))
    return e[0:1, :] / jnp.sum(e, axis=0, keepdims=True)


def _pre_kernel(x_ref, n1_ref, wgu_ref, wd_ref, nmix_ref, win_ref, lng_ref, lnb_ref,
                lbf_ref, lbb_ref,
                x1_ref, gu_ref, gv_ref, qs_ref, i_ref, ff_ref, fb_ref, g_ref,
                hb_ref, acc_ref):
    x = x_ref[...]
    hb_ref[...] = _rms(x, n1_ref[...]).astype(BF16)
    _swiglu_into(acc_ref, hb_ref, wgu_ref, wd_ref)
    x1 = x + 0.5 * acc_ref[...]
    x1_ref[...] = x1
    hb_ref[...] = _rms(x1, nmix_ref[...]).astype(BF16)

    def proj(j):
        return jnp.dot(hb_ref[...], win_ref[j], preferred_element_type=F32)

    gu_ref[...] = _gelu(proj(0))
    v = _gelu(proj(1))
    mu = jnp.mean(v, axis=-1, keepdims=True)
    vc = v - mu
    gv_ref[...] = (vc * lax.rsqrt(jnp.mean(vc * vc, axis=-1, keepdims=True) + EPS)
                   * lng_ref[...] + lnb_ref[...])
    qs_ref[...] = _silu(proj(2))
    i_ref[...] = proj(3)
    lbf = _lower_bound(lbf_ref[...])
    ff_ref[...] = lbf + (1.0 - lbf) * jax.nn.sigmoid(proj(4))
    lbb = _lower_bound(lbb_ref[...])
    fb_ref[...] = lbb + (1.0 - lbb) * jax.nn.sigmoid(proj(5))
    g_ref[...] = proj(6)


def _hgrn_chunk(qs, v, f, st_ref, dmat, mask, ref_is_first_half):
    lf = jnp.log(f)
    k = 1.0 - f
    hi = lf.astype(BF16)
    lo = (lf - hi.astype(F32)).astype(BF16)
    x = jnp.dot(dmat, jnp.concatenate([hi, lo], axis=0), preferred_element_type=F32)
    sum_a = jnp.sum(lf[:HALF], axis=0, keepdims=True)
    sum_b = jnp.sum(lf[HALF:], axis=0, keepdims=True)
    r_in, r_out = (sum_a, sum_b) if ref_is_first_half else (sum_b, sum_a)
    qt = qs * jnp.exp(x)
    kt = k * jnp.exp(-x)
    qh = (qt * jnp.exp(r_in)).astype(BF16)
    kh = (kt * jnp.exp(r_out)).astype(BF16)
    dec = jnp.exp(r_in + r_out)
    qt = qt.astype(BF16)
    kt = kt.astype(BF16)
    vb = v.astype(BF16)
    outs = []
    for h in range(N_HEADS):
        sl = slice(h * HEAD, (h + 1) * HEAD)
        p = lax.dot_general(qt[:, sl], kt[:, sl], NT_DIMS, preferred_element_type=F32)
        p = jnp.where(mask, p, 0.0).astype(BF16)
        st = st_ref[h]
        o = jnp.dot(p, vb[:, sl], preferred_element_type=F32)
        o += lax.dot_general(qh[:, sl], st.astype(BF16), NT_DIMS, preferred_element_type=F32)
        outs.append(o)
        st_ref[h] = st * dec[:, sl] + lax.dot_general(vb[:, sl], kh[:, sl], TN_DIMS,
                                                      preferred_element_type=F32)
    return jnp.concatenate(outs, axis=-1)


def _mixer_kernel(gu_ref, gv_ref, qsf_ref, if_ref, ff_ref, qsb_ref, ib_ref, fb_ref,
                  ws_ref, bs_ref, gon_ref, dfw_ref, dbw_ref,
                  yg_ref, of_ref, ob_ref,
                  sf_ref, sb_ref):
    @pl.when(pl.program_id(1) == 0)
    def _():
        sf_ref[...] = jnp.zeros_like(sf_ref)
        sb_ref[...] = jnp.zeros_like(sb_ref)

    row = lax.broadcasted_iota(jnp.int32, (CHUNK, CHUNK), 0)
    col = lax.broadcasted_iota(jnp.int32, (CHUNK, CHUNK), 1)
    causal = col <= row
    anticausal = col >= row

    for c in range(MIX_CHUNKS):
        rows = slice(c * CHUNK, (c + 1) * CHUNK)
        gvb = gv_ref[rows, :].astype(BF16)
        mixed = []
        for h in range(N_HEADS):
            sl = slice(h * HEAD, (h + 1) * HEAD)
            mixed.append(jnp.dot(ws_ref[h], gvb[:, sl], preferred_element_type=F32)
                         + bs_ref[:, h:h + 1])
        y = gu_ref[rows, :] * jnp.concatenate(mixed, axis=-1)
        yg_ref[rows, :] = _rms(y, gon_ref[...])
        of_ref[rows, :] = _hgrn_chunk(qsf_ref[rows, :], if_ref[rows, :], ff_ref[rows, :],
                                      sf_ref, dfw_ref[...], causal, True)
        rrows = slice((MIX_CHUNKS - 1 - c) * CHUNK, (MIX_CHUNKS - c) * CHUNK)
        ob_ref[rrows, :] = _hgrn_chunk(qsb_ref[rrows, :], ib_ref[rrows, :], fb_ref[rrows, :],
                                       sb_ref, dbw_ref[...], anticausal, False)


def _post_kernel(x1_ref, yg_ref, of_ref, ob_ref, g_ref, hon_ref, wout_ref, n2_ref,
                 wgu_ref, wd_ref, nf_ref,
                 out_ref,
                 hb_ref, acc_ref):
    o = of_ref[...] + ob_ref[...]
    heads = []
    for h in range(N_HEADS):
        sl = slice(h * HEAD, (h + 1) * HEAD)
        heads.append(_rms(o[:, sl], hon_ref[:, sl]))
    yh = jnp.concatenate(heads, axis=-1) * _silu(g_ref[...])
    x2 = (x1_ref[...]
          + jnp.dot(yg_ref[...].astype(BF16), wout_ref[0], preferred_element_type=F32)
          + jnp.dot(yh.astype(BF16), wout_ref[1], preferred_element_type=F32))
    hb_ref[...] = _rms(x2, n2_ref[...]).astype(BF16)
    _swiglu_into(acc_ref, hb_ref, wgu_ref, wd_ref)
    x3 = x2 + 0.5 * acc_ref[...]
    out_ref[...] = _rms(x3, nf_ref[...])


def _resident(shape):
    nd = len(shape)
    return pl.BlockSpec(shape, lambda *_: (0,) * nd, pipeline_mode=pl.Buffered(1))


def _pack_ffn(w_gate, w_up, w_down):
    wg = w_gate.astype(BF16).reshape(D_MODEL, FF_STEPS, FF_COLS)
    wu = w_up.astype(BF16).reshape(D_MODEL, FF_STEPS, FF_COLS)
    wgu = jnp.concatenate([wg, wu], axis=-1).transpose(1, 0, 2)
    wd = w_down.astype(BF16).reshape(FF_STEPS, FF_COLS, D_MODEL)
    return wgu, wd


def _decay_matrices():
    t = np.arange(CHUNK)[:, None]
    j = np.arange(CHUNK)[None, :]
    d_fw = (j <= t).astype(np.float32) - (j < HALF).astype(np.float32)
    d_bw = (j >= t).astype(np.float32) - (j >= HALF).astype(np.float32)
    dup = lambda d: jnp.asarray(np.concatenate([d, d], axis=1), dtype=BF16)
    return dup(d_fw), dup(d_bw)


def _row(v):
    return v.reshape(1, -1).astype(F32)


def _params(sem):
    return pltpu.CompilerParams(dimension_semantics=sem, vmem_limit_bytes=VMEM_LIMIT)


def _trunk(x, w):
    batch, seq, _ = x.shape
    n_tok = batch * seq
    assert n_tok % TOK_PRE == 0 and n_tok % TOK_POST == 0 and seq % TOK_MIX == 0
    xf = x.reshape(n_tok, D_MODEL)

    tok = lambda t, d: pl.BlockSpec((t, d), lambda i: (i, 0))
    half_shape = jax.ShapeDtypeStruct((n_tok, D_HGRN), F32)

    pre_out = pl.pallas_call(
        _pre_kernel,
        grid=(n_tok // TOK_PRE,),
        in_specs=[tok(TOK_PRE, D_MODEL), _resident((1, D_MODEL)),
                  _resident(w["wgu1"].shape), _resident(w["wd1"].shape),
                  _resident((1, D_MODEL)), _resident(w["win"].shape),
                  _resident((1, D_GMLP)), _resident((1, D_GMLP)),
                  _resident(w["lbf"].shape), _resident(w["lbb"].shape)],
        out_specs=[tok(TOK_PRE, D_MODEL)] + [tok(TOK_PRE, D_HGRN)] * 7,
        out_shape=[jax.ShapeDtypeStruct((n_tok, D_MODEL), F32)] + [half_shape] * 7,
        scratch_shapes=[pltpu.VMEM((TOK_PRE, D_MODEL), BF16), pltpu.VMEM((TOK_PRE, D_MODEL), F32)],
        compiler_params=_params(("parallel",)),
        name="pre",
    )(xf, w["n1"], w["wgu1"], w["wd1"], w["nmix"], w["win"], w["lng"], w["lnb"], w["lbf"], w["lbb"])
    x1, gu, gv, qs, iv, ff, fb, g = pre_out

    n_blk = seq // TOK_MIX
    seq3 = lambda a: a.reshape(batch, seq, D_HGRN)
    fwd = pl.BlockSpec((None, TOK_MIX, D_HGRN), lambda b, c: (b, c, 0))
    bwd = pl.BlockSpec((None, TOK_MIX, D_HGRN), lambda b, c: (b, n_blk - 1 - c, 0))
    seq_shape = jax.ShapeDtypeStruct((batch, seq, D_HGRN), F32)
    yg, of, ob = pl.pallas_call(
        _mixer_kernel,
        grid=(batch, n_blk),
        in_specs=[fwd, fwd, fwd, fwd, fwd, bwd, bwd, bwd,
                  _resident(w["ws"].shape), _resident(w["bs"].shape), _resident((1, D_GMLP)),
                  _resident(w["dfw"].shape), _resident(w["dbw"].shape)],
        out_specs=[fwd, fwd, bwd],
        out_shape=[seq_shape] * 3,
        scratch_shapes=[pltpu.VMEM((N_HEADS, HEAD, HEAD), F32), pltpu.VMEM((N_HEADS, HEAD, HEAD), F32)],
        compiler_params=_params(("parallel", "arbitrary")),
        name="mixer",
    )(seq3(gu), seq3(gv), seq3(qs), seq3(iv), seq3(ff), seq3(qs), seq3(iv), seq3(fb),
      w["ws"], w["bs"], w["gon"], w["dfw"], w["dbw"])

    flat = lambda a: a.reshape(n_tok, D_HGRN)
    out = pl.pallas_call(
        _post_kernel,
        grid=(n_tok // TOK_POST,),
        in_specs=[tok(TOK_POST, D_MODEL)] + [tok(TOK_POST, D_HGRN)] * 4
                 + [_resident((1, D_HGRN)), _resident(w["wout"].shape), _resident((1, D_MODEL)),
                    _resident(w["wgu2"].shape), _resident(w["wd2"].shape), _resident((1, D_MODEL))],
        out_specs=tok(TOK_POST, D_MODEL),
        out_shape=jax.ShapeDtypeStruct((n_tok, D_MODEL), F32),
        scratch_shapes=[pltpu.VMEM((TOK_POST, D_MODEL), BF16), pltpu.VMEM((TOK_POST, D_MODEL), F32)],
        compiler_params=_params(("parallel",)),
        name="post",
    )(x1, flat(yg), flat(of), flat(ob), g, w["hon"], w["wout"], w["n2"], w["wgu2"], w["wd2"], w["nf"])
    return out.reshape(batch, seq, D_MODEL)


def kernel(x_prompt, x_sample, ffn1_norm, ffn1_w_gate, ffn1_w_up, ffn1_w_down, mix_norm, w_in,
           gmlp_ln_g, gmlp_ln_b, gmlp_w_s, gmlp_b_s, gmlp_out_norm,
           hgrn_lb_fwd, hgrn_lb_bwd, hgrn_out_norm, w_out,
           ffn2_norm, ffn2_w_gate, ffn2_w_up, ffn2_w_down, final_norm):
    wgu1, wd1 = _pack_ffn(ffn1_w_gate[0], ffn1_w_up[0], ffn1_w_down[0])
    wgu2, wd2 = _pack_ffn(ffn2_w_gate[0], ffn2_w_up[0], ffn2_w_down[0])
    dfw, dbw = _decay_matrices()
    w = dict(
        n1=_row(ffn1_norm[0]), wgu1=wgu1, wd1=wd1, nmix=_row(mix_norm[0]),
        win=w_in[0].astype(BF16).reshape(D_MODEL, D_IN // D_HGRN, D_HGRN).transpose(1, 0, 2),
        lng=_row(gmlp_ln_g[0]), lnb=_row(gmlp_ln_b[0]),
        lbf=hgrn_lb_fwd.astype(F32), lbb=hgrn_lb_bwd.astype(F32),
        ws=gmlp_w_s[0].astype(BF16), bs=gmlp_b_s[0].T.astype(F32), gon=_row(gmlp_out_norm[0]),
        dfw=dfw, dbw=dbw,
        hon=_row(hgrn_out_norm[0]), wout=w_out[0].astype(BF16).reshape(2, D_GMLP, D_MODEL),
        n2=_row(ffn2_norm[0]), wgu2=wgu2, wd2=wd2, nf=_row(final_norm),
    )
    return _trunk(x_prompt, w), _trunk(x_sample, w)
```

```python
import functools

import jax
import jax.numpy as jnp
import numpy as np
from jax import lax
from jax.experimental import pallas as pl
from jax.experimental.pallas import tpu as pltpu

F32 = jnp.float32
BF16 = jnp.bfloat16

D_MODEL = 1024
D_GMLP = 512
D_HGRN = 512
N_HEADS = 4
HEAD = 128
CHUNK = 128
HALF = CHUNK // 2
D_FF = 2816
D_IN = 2 * D_GMLP + 5 * D_HGRN
EPS = 1e-6

FF_COLS = 256
FF_STEPS = D_FF // FF_COLS
TOK_PRE = 256
TOK_POST = 256
TOK_MIX = 512
MIX_CHUNKS = TOK_MIX // CHUNK
VMEM_LIMIT = 56 * 1024 * 1024

SQRT_HALF = float(np.sqrt(0.5).astype(np.float32))

NT_DIMS = (((1,), (1,)), ((), ()))
TN_DIMS = (((0,), (0,)), ((), ()))


def _rms(x, gain):
    return x * lax.rsqrt(jnp.mean(x * x, axis=-1, keepdims=True) + EPS) * gain


def _silu(x):
    return x * jax.nn.sigmoid(x)


def _gelu(x):
    return 0.5 * x * (1.0 + lax.erf(x * SQRT_HALF))


def _swiglu_into(acc_ref, hb_ref, wg_ref, wu_ref, wd_ref):
    acc_ref[...] = jnp.zeros_like(acc_ref)
    for c in range(FF_STEPS):
        cols = slice(c * FF_COLS, (c + 1) * FF_COLS)
        g = jnp.dot(hb_ref[...], wg_ref[:, cols], preferred_element_type=F32)
        u = jnp.dot(hb_ref[...], wu_ref[:, cols], preferred_element_type=F32)
        a = (_silu(g) * u).astype(BF16)
        acc_ref[...] += jnp.dot(a, wd_ref[cols, :], preferred_element_type=F32)


def _lower_bound(p):
    e = jnp.exp(p - jnp.max(p, axis=0, keepdims=True))
    return e[0:1, :] / jnp.sum(e, axis=0, keepdims=True)


def _pre_kernel(x_ref, n1_ref, wg_ref, wu_ref, wd_ref, nmix_ref, win_ref, lng_ref, lnb_ref,
                lbf_ref, lbb_ref,
                x1_ref, gu_ref, gv_ref, qs_ref, i_ref, ff_ref, fb_ref, g_ref,
                hb_ref, acc_ref):
    x = x_ref[...]
    hb_ref[...] = _rms(x, n1_ref[...]).astype(BF16)
    _swiglu_into(acc_ref, hb_ref, wg_ref, wu_ref, wd_ref)
    x1 = x + 0.5 * acc_ref[...]
    x1_ref[...] = x1
    hb_ref[...] = _rms(x1, nmix_ref[...]).astype(BF16)

    def proj(j):
        return jnp.dot(hb_ref[...], win_ref[:, j * D_HGRN:(j + 1) * D_HGRN],
                       preferred_element_type=F32)

    gu_ref[...] = _gelu(proj(0))
    v = _gelu(proj(1))
    mu = jnp.mean(v, axis=-1, keepdims=True)
    vc = v - mu
    gv_ref[...] = (vc * lax.rsqrt(jnp.mean(vc * vc, axis=-1, keepdims=True) + EPS)
                   * lng_ref[...] + lnb_ref[...])
    qs_ref[...] = _silu(proj(2))
    i_ref[...] = proj(3)
    lbf = _lower_bound(lbf_ref[...])
    ff_ref[...] = lbf + (1.0 - lbf) * jax.nn.sigmoid(proj(4))
    lbb = _lower_bound(lbb_ref[...])
    fb_ref[...] = lbb + (1.0 - lbb) * jax.nn.sigmoid(proj(5))
    g_ref[...] = proj(6)


def _hgrn_chunk(qs, v, f, st_ref, dmat, mask, ref_is_first_half):
    lf = jnp.log(f)
    k = 1.0 - f
    hi = lf.astype(BF16)
    lo = (lf - hi.astype(F32)).astype(BF16)
    x = jnp.dot(dmat, jnp.concatenate([hi, lo], axis=0), preferred_element_type=F32)
    sum_a = jnp.sum(lf[:HALF], axis=0, keepdims=True)
    sum_b = jnp.sum(lf[HALF:], axis=0, keepdims=True)
    r_in, r_out = (sum_a, sum_b) if ref_is_first_half else (sum_b, sum_a)
    qt = qs * jnp.exp(x)
    kt = k * jnp.exp(-x)
    qh = (qt * jnp.exp(r_in)).astype(BF16)
    kh = (kt * jnp.exp(r_out)).astype(BF16)
    dec = jnp.exp(r_in + r_out)
    qt = qt.astype(BF16)
    kt = kt.astype(BF16)
    vb = v.astype(BF16)
    outs = []
    for h in range(N_HEADS):
        sl = slice(h * HEAD, (h + 1) * HEAD)
        p = lax.dot_general(qt[:, sl], kt[:, sl], NT_DIMS, preferred_element_type=F32)
        p = jnp.where(mask, p, 0.0).astype(BF16)
        st = st_ref[h]
        o = jnp.dot(p, vb[:, sl], preferred_element_type=F32)
        o += lax.dot_general(qh[:, sl], st.astype(BF16), NT_DIMS, preferred_element_type=F32)
        outs.append(o)
        st_ref[h] = st * dec[:, sl] + lax.dot_general(vb[:, sl], kh[:, sl], TN_DIMS,
                                                      preferred_element_type=F32)
    return jnp.concatenate(outs, axis=-1)


def _mixer_kernel(gu_ref, gv_ref, qsf_ref, if_ref, ff_ref, qsb_ref, ib_ref, fb_ref,
                  ws_ref, bs_ref, gon_ref, dfw_ref, dbw_ref,
                  yg_ref, of_ref, ob_ref,
                  sf_ref, sb_ref):
    @pl.when(pl.program_id(1) == 0)
    def _():
        sf_ref[...] = jnp.zeros_like(sf_ref)
        sb_ref[...] = jnp.zeros_like(sb_ref)

    row = lax.broadcasted_iota(jnp.int32, (CHUNK, CHUNK), 0)
    col = lax.broadcasted_iota(jnp.int32, (CHUNK, CHUNK), 1)
    causal = col <= row
    anticausal = col >= row

    for c in range(MIX_CHUNKS):
        rows = slice(c * CHUNK, (c + 1) * CHUNK)
        gvb = gv_ref[rows, :].astype(BF16)
        mixed = []
        for h in range(N_HEADS):
            sl = slice(h * HEAD, (h + 1) * HEAD)
            mixed.append(jnp.dot(ws_ref[h], gvb[:, sl], preferred_element_type=F32)
                         + bs_ref[:, h:h + 1])
        y = gu_ref[rows, :] * jnp.concatenate(mixed, axis=-1)
        yg_ref[rows, :] = _rms(y, gon_ref[...])
        of_ref[rows, :] = _hgrn_chunk(qsf_ref[rows, :], if_ref[rows, :], ff_ref[rows, :],
                                      sf_ref, dfw_ref[...], causal, True)
        rrows = slice((MIX_CHUNKS - 1 - c) * CHUNK, (MIX_CHUNKS - c) * CHUNK)
        ob_ref[rrows, :] = _hgrn_chunk(qsb_ref[rrows, :], ib_ref[rrows, :], fb_ref[rrows, :],
                                       sb_ref, dbw_ref[...], anticausal, False)


def _post_kernel(x1_ref, yg_ref, of_ref, ob_ref, g_ref, hon_ref, wout_ref, n2_ref,
                 wg_ref, wu_ref, wd_ref, nf_ref,
                 out_ref,
                 hb_ref, acc_ref):
    o = of_ref[...] + ob_ref[...]
    heads = []
    for h in range(N_HEADS):
        sl = slice(h * HEAD, (h + 1) * HEAD)
        heads.append(_rms(o[:, sl], hon_ref[:, sl]))
    yh = jnp.concatenate(heads, axis=-1) * _silu(g_ref[...])
    x2 = (x1_ref[...]
          + jnp.dot(yg_ref[...].astype(BF16), wout_ref[:D_GMLP, :], preferred_element_type=F32)
          + jnp.dot(yh.astype(BF16), wout_ref[D_GMLP:, :], preferred_element_type=F32))
    hb_ref[...] = _rms(x2, n2_ref[...]).astype(BF16)
    _swiglu_into(acc_ref, hb_ref, wg_ref, wu_ref, wd_ref)
    x3 = x2 + 0.5 * acc_ref[...]
    out_ref[...] = _rms(x3, nf_ref[...])


def _resident(shape):
    nd = len(shape)
    return pl.BlockSpec(shape, lambda *_: (0,) * nd, pipeline_mode=pl.Buffered(1))


def _decay_matrices():
    t = np.arange(CHUNK)[:, None]
    j = np.arange(CHUNK)[None, :]
    d_fw = (j <= t).astype(np.float32) - (j < HALF).astype(np.float32)
    d_bw = (j >= t).astype(np.float32) - (j >= HALF).astype(np.float32)
    dup = lambda d: jnp.asarray(np.concatenate([d, d], axis=1), dtype=BF16)
    return dup(d_fw), dup(d_bw)


def _row(v):
    return v.reshape(1, -1).astype(F32)


def _params(sem):
    return pltpu.CompilerParams(dimension_semantics=sem, vmem_limit_bytes=VMEM_LIMIT)


def _trunk(x, w):
    batch, seq, _ = x.shape
    n_tok = batch * seq
    assert n_tok % TOK_PRE == 0 and n_tok % TOK_POST == 0 and seq % TOK_MIX == 0
    xf = x.reshape(n_tok, D_MODEL)

    tok = lambda t, d: pl.BlockSpec((t, d), lambda i: (i, 0))
    half_shape = jax.ShapeDtypeStruct((n_tok, D_HGRN), F32)

    pre_out = pl.pallas_call(
        _pre_kernel,
        grid=(n_tok // TOK_PRE,),
        in_specs=[tok(TOK_PRE, D_MODEL), _resident((1, D_MODEL)),
                  _resident((D_MODEL, D_FF)), _resident((D_MODEL, D_FF)), _resident((D_FF, D_MODEL)),
                  _resident((1, D_MODEL)), _resident((D_MODEL, D_IN)),
                  _resident((1, D_GMLP)), _resident((1, D_GMLP)),
                  _resident(w["lbf"].shape), _resident(w["lbb"].shape)],
        out_specs=[tok(TOK_PRE, D_MODEL)] + [tok(TOK_PRE, D_HGRN)] * 7,
        out_shape=[jax.ShapeDtypeStruct((n_tok, D_MODEL), F32)] + [half_shape] * 7,
        scratch_shapes=[pltpu.VMEM((TOK_PRE, D_MODEL), BF16), pltpu.VMEM((TOK_PRE, D_MODEL), F32)],
        compiler_params=_params(("parallel",)),
        name="pre",
    )(xf, w["n1"], w["wg1"], w["wu1"], w["wd1"], w["nmix"], w["win"], w["lng"], w["lnb"],
      w["lbf"], w["lbb"])
    x1, gu, gv, qs, iv, ff, fb, g = pre_out

    n_blk = seq // TOK_MIX
    seq3 = lambda a: a.reshape(batch, seq, D_HGRN)
    fwd = pl.BlockSpec((None, TOK_MIX, D_HGRN), lambda b, c: (b, c, 0))
    bwd = pl.BlockSpec((None, TOK_MIX, D_HGRN), lambda b, c: (b, n_blk - 1 - c, 0))
    seq_shape = jax.ShapeDtypeStruct((batch, seq, D_HGRN), F32)
    yg, of, ob = pl.pallas_call(
        _mixer_kernel,
        grid=(batch, n_blk),
        in_specs=[fwd, fwd, fwd, fwd, fwd, bwd, bwd, bwd,
                  _resident(w["ws"].shape), _resident(w["bs"].shape), _resident((1, D_GMLP)),
                  _resident(w["dfw"].shape), _resident(w["dbw"].shape)],
        out_specs=[fwd, fwd, bwd],
        out_shape=[seq_shape] * 3,
        scratch_shapes=[pltpu.VMEM((N_HEADS, HEAD, HEAD), F32), pltpu.VMEM((N_HEADS, HEAD, HEAD), F32)],
        compiler_params=_params(("parallel", "arbitrary")),
        name="mixer",
    )(seq3(gu), seq3(gv), seq3(qs), seq3(iv), seq3(ff), seq3(qs), seq3(iv), seq3(fb),
      w["ws"], w["bs"], w["gon"], w["dfw"], w["dbw"])

    flat = lambda a: a.reshape(n_tok, D_HGRN)
    out = pl.pallas_call(
        _post_kernel,
        grid=(n_tok // TOK_POST,),
        in_specs=[tok(TOK_POST, D_MODEL)] + [tok(TOK_POST, D_HGRN)] * 4
                 + [_resident((1, D_HGRN)), _resident((D_MODEL, D_MODEL)), _resident((1, D_MODEL)),
                    _resident((D_MODEL, D_FF)), _resident((D_MODEL, D_FF)), _resident((D_FF, D_MODEL)),
                    _resident((1, D_MODEL))],
        out_specs=tok(TOK_POST, D_MODEL),
        out_shape=jax.ShapeDtypeStruct((n_tok, D_MODEL), F32),
        scratch_shapes=[pltpu.VMEM((TOK_POST, D_MODEL), BF16), pltpu.VMEM((TOK_POST, D_MODEL), F32)],
        compiler_params=_params(("parallel",)),
        name="post",
    )(x1, flat(yg), flat(of), flat(ob), g, w["hon"], w["wout"], w["n2"],
      w["wg2"], w["wu2"], w["wd2"], w["nf"])
    return out.reshape(batch, seq, D_MODEL)


def kernel(x_prompt, x_sample, ffn1_norm, ffn1_w_gate, ffn1_w_up, ffn1_w_down, mix_norm, w_in,
           gmlp_ln_g, gmlp_ln_b, gmlp_w_s, gmlp_b_s, gmlp_out_norm,
           hgrn_lb_fwd, hgrn_lb_bwd, hgrn_out_norm, w_out,
           ffn2_norm, ffn2_w_gate, ffn2_w_up, ffn2_w_down, final_norm):
    dfw, dbw = _decay_matrices()
    w = dict(
        n1=_row(ffn1_norm[0]), wg1=ffn1_w_gate[0].astype(BF16), wu1=ffn1_w_up[0].astype(BF16),
        wd1=ffn1_w_down[0].astype(BF16), nmix=_row(mix_norm[0]), win=w_in[0].astype(BF16),
        lng=_row(gmlp_ln_g[0]), lnb=_row(gmlp_ln_b[0]),
        lbf=hgrn_lb_fwd.astype(F32), lbb=hgrn_lb_bwd.astype(F32),
        ws=gmlp_w_s[0].astype(BF16), bs=gmlp_b_s[0].T.astype(F32), gon=_row(gmlp_out_norm[0]),
        dfw=dfw, dbw=dbw,
        hon=_row(hgrn_out_norm[0]), wout=w_out[0].astype(BF16),
        n2=_row(ffn2_norm[0]), wg2=ffn2_w_gate[0].astype(BF16), wu2=ffn2_w_up[0].astype(BF16),
        wd2=ffn2_w_down[0].astype(BF16), nf=_row(final_norm),
    )
    return _trunk(x_prompt, w), _trunk(x_sample, w)
```

```python
import functools

import jax
import jax.numpy as jnp
import numpy as np
from jax import lax
from jax.experimental import pallas as pl
from jax.experimental.pallas import tpu as pltpu

F32 = jnp.float32
BF16 = jnp.bfloat16

D_MODEL = 1024
D_GMLP = 512
D_HGRN = 512
N_HEADS = 4
HEAD = 128
CHUNK = 128
HALF = CHUNK // 2
D_FF = 2816
D_IN = 2 * D_GMLP + 5 * D_HGRN
EPS = 1e-6

FF_COLS = 256
FF_STEPS = D_FF // FF_COLS
TOK_PRE = 512
TOK_POST = 512
TOK_MIX = 512
MIX_CHUNKS = TOK_MIX // CHUNK
N_LEVELS = 7
MAX_EXP_ARG = 80.0
VMEM_LIMIT = 56 * 1024 * 1024

SQRT_HALF = float(np.sqrt(0.5).astype(np.float32))

NT_DIMS = (((1,), (1,)), ((), ()))
TN_DIMS = (((0,), (0,)), ((), ()))


def _rms(x, gain):
    return x * lax.rsqrt(jnp.mean(x * x, axis=-1, keepdims=True) + EPS) * gain


def _silu(x):
    return x * jax.nn.sigmoid(x)


def _gelu(x):
    return 0.5 * x * (1.0 + lax.erf(x * SQRT_HALF))


def _swiglu_into(acc_ref, hb_ref, wg_ref, wu_ref, wd_ref):
    acc_ref[...] = jnp.zeros_like(acc_ref)
    for c in range(FF_STEPS):
        cols = slice(c * FF_COLS, (c + 1) * FF_COLS)
        g = jnp.dot(hb_ref[...], wg_ref[:, cols], preferred_element_type=F32)
        u = jnp.dot(hb_ref[...], wu_ref[:, cols], preferred_element_type=F32)
        a = (_silu(g) * u).astype(BF16)
        acc_ref[...] += jnp.dot(a, wd_ref[cols, :], preferred_element_type=F32)


def _lower_bound(p):
    e = jnp.exp(p - jnp.max(p, axis=0, keepdims=True))
    return e[0:1, :] / jnp.sum(e, axis=0, keepdims=True)


def _pre_kernel(x_ref, n1_ref, wg_ref, wu_ref, wd_ref, nmix_ref, win_ref, lng_ref, lnb_ref,
                lbf_ref, lbb_ref,
                x1_ref, gu_ref, gv_ref, qs_ref, i_ref, ff_ref, fb_ref, g_ref,
                hb_ref, acc_ref):
    x = x_ref[...]
    hb_ref[...] = _rms(x, n1_ref[...]).astype(BF16)
    _swiglu_into(acc_ref, hb_ref, wg_ref, wu_ref, wd_ref)
    x1 = x + 0.5 * acc_ref[...]
    x1_ref[...] = x1
    hb_ref[...] = _rms(x1, nmix_ref[...]).astype(BF16)

    def proj(j):
        return jnp.dot(hb_ref[...], win_ref[:, j * D_HGRN:(j + 1) * D_HGRN],
                       preferred_element_type=F32)

    gu_ref[...] = _gelu(proj(0))
    v = _gelu(proj(1))
    mu = jnp.mean(v, axis=-1, keepdims=True)
    vc = v - mu
    gv_ref[...] = (vc * lax.rsqrt(jnp.mean(vc * vc, axis=-1, keepdims=True) + EPS)
                   * lng_ref[...] + lnb_ref[...])
    qs_ref[...] = _silu(proj(2))
    i_ref[...] = proj(3)
    lbf = _lower_bound(lbf_ref[...])
    ff_ref[...] = lbf + (1.0 - lbf) * jax.nn.sigmoid(proj(4))
    lbb = _lower_bound(lbb_ref[...])
    fb_ref[...] = lbb + (1.0 - lbb) * jax.nn.sigmoid(proj(5))
    g_ref[...] = proj(6)


def _hgrn_chunk(qs, v, f, st_ref, dm_ref, forward, bounded):
    lf = jnp.log(f)
    k = 1.0 - f
    hi = lf.astype(BF16)
    lo = (lf - hi.astype(F32)).astype(BF16)
    hilo = jnp.concatenate([hi, lo], axis=0)
    cum = lambda i: jnp.dot(dm_ref[i], hilo, preferred_element_type=F32)
    row = lax.broadcasted_iota(jnp.int32, (CHUNK, CHUNK), 0)
    col = lax.broadcasted_iota(jnp.int32, (CHUNK, CHUNK), 1)
    heads = [slice(h * HEAD, (h + 1) * HEAD) for h in range(N_HEADS)]
    vb = v.astype(BF16)

    if not bounded:
        x = cum(0)
        sum_a = jnp.sum(lf[:HALF], axis=0, keepdims=True)
        sum_b = jnp.sum(lf[HALF:], axis=0, keepdims=True)
        r_in, r_out = (sum_a, sum_b) if forward else (sum_b, sum_a)
        qt = qs * jnp.exp(x)
        kt = k * jnp.exp(-x)
        qh = (qt * jnp.exp(r_in)).astype(BF16)
        kh = (kt * jnp.exp(r_out)).astype(BF16)
        dec = jnp.exp(r_in + r_out)
        qt = qt.astype(BF16)
        kt = kt.astype(BF16)
        visible = (col <= row) if forward else (col >= row)
        scores = [jnp.where(visible,
                            lax.dot_general(qt[:, sl], kt[:, sl], NT_DIMS, preferred_element_type=F32),
                            0.0) for sl in heads]
        same_pos = [None] * N_HEADS
    else:
        qh = (qs * jnp.exp(cum(N_LEVELS))).astype(BF16)
        kh = (k * jnp.exp(cum(N_LEVELS + 1))).astype(BF16)
        dec = jnp.exp(jnp.sum(lf, axis=0, keepdims=True))
        tok = lax.broadcasted_iota(jnp.int32, (CHUNK, D_HGRN), 0)
        scores = [jnp.zeros((CHUNK, CHUNK), F32)] * N_HEADS
        for lvl in range(N_LEVELS):
            m = 1 << lvl
            is_query = ((tok & m) != 0) if forward else ((tok & m) == 0)
            z = (jnp.where(is_query, qs, k) * jnp.exp(-jnp.abs(cum(lvl)))).astype(BF16)
            pair = (((row ^ col) >> lvl) == 1) & ((col < row) if forward else (col > row))
            scores = [sc + jnp.where(pair,
                                     lax.dot_general(z[:, sl], z[:, sl], NT_DIMS,
                                                     preferred_element_type=F32), 0.0)
                      for sc, sl in zip(scores, heads)]
        same_pos = [jnp.sum(qs[:, sl] * k[:, sl], axis=-1, keepdims=True) * v[:, sl] for sl in heads]

    outs = []
    for h, sl in enumerate(heads):
        st = st_ref[h]
        o = jnp.dot(scores[h].astype(BF16), vb[:, sl], preferred_element_type=F32)
        o += lax.dot_general(qh[:, sl], st.astype(BF16), NT_DIMS, preferred_element_type=F32)
        if same_pos[h] is not None:
            o += same_pos[h]
        outs.append(o)
        st_ref[h] = st * dec[:, sl] + lax.dot_general(vb[:, sl], kh[:, sl], TN_DIMS,
                                                      preferred_element_type=F32)
    return jnp.concatenate(outs, axis=-1)


def _mixer_kernel(gu_ref, gv_ref, qsf_ref, if_ref, ff_ref, qsb_ref, ib_ref, fb_ref,
                  ws_ref, bs_ref, gon_ref, dfw_ref, dbw_ref,
                  yg_ref, of_ref, ob_ref,
                  sf_ref, sb_ref, *, bounded):
    @pl.when(pl.program_id(1) == 0)
    def _():
        sf_ref[...] = jnp.zeros_like(sf_ref)
        sb_ref[...] = jnp.zeros_like(sb_ref)

    for c in range(MIX_CHUNKS):
        rows = slice(c * CHUNK, (c + 1) * CHUNK)
        gvb = gv_ref[rows, :].astype(BF16)
        mixed = []
        for h in range(N_HEADS):
            sl = slice(h * HEAD, (h + 1) * HEAD)
            mixed.append(jnp.dot(ws_ref[h], gvb[:, sl], preferred_element_type=F32)
                         + bs_ref[:, h:h + 1])
        y = gu_ref[rows, :] * jnp.concatenate(mixed, axis=-1)
        yg_ref[rows, :] = _rms(y, gon_ref[...])
        of_ref[rows, :] = _hgrn_chunk(qsf_ref[rows, :], if_ref[rows, :], ff_ref[rows, :],
                                      sf_ref, dfw_ref, True, bounded)
        rrows = slice((MIX_CHUNKS - 1 - c) * CHUNK, (MIX_CHUNKS - c) * CHUNK)
        ob_ref[rrows, :] = _hgrn_chunk(qsb_ref[rrows, :], ib_ref[rrows, :], fb_ref[rrows, :],
                                       sb_ref, dbw_ref, False, bounded)


def _post_kernel(x1_ref, yg_ref, of_ref, ob_ref, g_ref, hon_ref, wout_ref, n2_ref,
                 wg_ref, wu_ref, wd_ref, nf_ref,
                 out_ref,
                 hb_ref, acc_ref):
    o = of_ref[...] + ob_ref[...]
    heads = []
    for h in range(N_HEADS):
        sl = slice(h * HEAD, (h + 1) * HEAD)
        heads.append(_rms(o[:, sl], hon_ref[:, sl]))
    yh = jnp.concatenate(heads, axis=-1) * _silu(g_ref[...])
    x2 = (x1_ref[...]
          + jnp.dot(yg_ref[...].astype(BF16), wout_ref[:D_GMLP, :], preferred_element_type=F32)
          + jnp.dot(yh.astype(BF16), wout_ref[D_GMLP:, :], preferred_element_type=F32))
    hb_ref[...] = _rms(x2, n2_ref[...]).astype(BF16)
    _swiglu_into(acc_ref, hb_ref, wg_ref, wu_ref, wd_ref)
    x3 = x2 + 0.5 * acc_ref[...]
    out_ref[...] = _rms(x3, nf_ref[...])


def _resident(shape):
    nd = len(shape)
    return pl.BlockSpec(shape, lambda *_: (0,) * nd, pipeline_mode=pl.Buffered(1))


def _decay_matrices(bounded):
    t = np.arange(CHUNK)
    out = []
    for forward in (True, False):
        cums = (t[None, :] <= t[:, None]) if forward else (t[None, :] >= t[:, None])
        cums = cums.astype(np.float32)
        if bounded:
            mats = []
            for lvl in range(N_LEVELS):
                m = 1 << lvl
                ref = (t // (2 * m)) * (2 * m) + (m - 1 if forward else m)
                mats.append(cums - cums[ref])
            mats += [cums, 1.0 - cums]
        else:
            mats = [cums - cums[HALF - 1 if forward else HALF]]
        d = np.stack(mats)
        out.append(jnp.asarray(np.concatenate([d, d], axis=2), dtype=BF16))
    return out


def _row(v):
    return v.reshape(1, -1).astype(F32)


def _params(sem):
    return pltpu.CompilerParams(dimension_semantics=sem, vmem_limit_bytes=VMEM_LIMIT)


def _trunk(x, w):
    batch, seq, _ = x.shape
    n_tok = batch * seq
    assert n_tok % TOK_PRE == 0 and n_tok % TOK_POST == 0 and seq % TOK_MIX == 0
    xf = x.reshape(n_tok, D_MODEL)

    tok = lambda t, d: pl.BlockSpec((t, d), lambda i: (i, 0))
    half_shape = jax.ShapeDtypeStruct((n_tok, D_HGRN), F32)

    pre_out = pl.pallas_call(
        _pre_kernel,
        grid=(n_tok // TOK_PRE,),
        in_specs=[tok(TOK_PRE, D_MODEL), _resident((1, D_MODEL)),
                  _resident((D_MODEL, D_FF)), _resident((D_MODEL, D_FF)), _resident((D_FF, D_MODEL)),
                  _resident((1, D_MODEL)), _resident((D_MODEL, D_IN)),
                  _resident((1, D_GMLP)), _resident((1, D_GMLP)),
                  _resident(w["lbf"].shape), _resident(w["lbb"].shape)],
        out_specs=[tok(TOK_PRE, D_MODEL)] + [tok(TOK_PRE, D_HGRN)] * 7,
        out_shape=[jax.ShapeDtypeStruct((n_tok, D_MODEL), F32)] + [half_shape] * 7,
        scratch_shapes=[pltpu.VMEM((TOK_PRE, D_MODEL), BF16), pltpu.VMEM((TOK_PRE, D_MODEL), F32)],
        compiler_params=_params(("parallel",)),
        name="pre",
    )(xf, w["n1"], w["wg1"], w["wu1"], w["wd1"], w["nmix"], w["win"], w["lng"], w["lnb"],
      w["lbf"], w["lbb"])
    x1, gu, gv, qs, iv, ff, fb, g = pre_out

    n_blk = seq // TOK_MIX
    seq3 = lambda a: a.reshape(batch, seq, D_HGRN)
    fwd = pl.BlockSpec((None, TOK_MIX, D_HGRN), lambda b, c: (b, c, 0))
    bwd = pl.BlockSpec((None, TOK_MIX, D_HGRN), lambda b, c: (b, n_blk - 1 - c, 0))
    seq_shape = jax.ShapeDtypeStruct((batch, seq, D_HGRN), F32)

    def mixer(bounded):
        dfw, dbw = _decay_matrices(bounded)
        return pl.pallas_call(
            functools.partial(_mixer_kernel, bounded=bounded),
            grid=(batch, n_blk),
            in_specs=[fwd, fwd, fwd, fwd, fwd, bwd, bwd, bwd,
                      _resident(w["ws"].shape), _resident(w["bs"].shape), _resident((1, D_GMLP)),
                      _resident(dfw.shape), _resident(dbw.shape)],
            out_specs=[fwd, fwd, bwd],
            out_shape=[seq_shape] * 3,
            scratch_shapes=[pltpu.VMEM((N_HEADS, HEAD, HEAD), F32),
                            pltpu.VMEM((N_HEADS, HEAD, HEAD), F32)],
            compiler_params=_params(("parallel", "arbitrary")),
            name="mixer_bounded" if bounded else "mixer",
        )(seq3(gu), seq3(gv), seq3(qs), seq3(iv), seq3(ff), seq3(qs), seq3(iv), seq3(fb),
          w["ws"], w["bs"], w["gon"], dfw, dbw)

    yg, of, ob = lax.cond(w["lb_min"] >= np.exp(-MAX_EXP_ARG / HALF),
                          lambda: mixer(False), lambda: mixer(True))

    flat = lambda a: a.reshape(n_tok, D_HGRN)
    out = pl.pallas_call(
        _post_kernel,
        grid=(n_tok // TOK_POST,),
        in_specs=[tok(TOK_POST, D_MODEL)] + [tok(TOK_POST, D_HGRN)] * 4
                 + [_resident((1, D_HGRN)), _resident((D_MODEL, D_MODEL)), _resident((1, D_MODEL)),
                    _resident((D_MODEL, D_FF)), _resident((D_MODEL, D_FF)), _resident((D_FF, D_MODEL)),
                    _resident((1, D_MODEL))],
        out_specs=tok(TOK_POST, D_MODEL),
        out_shape=jax.ShapeDtypeStruct((n_tok, D_MODEL), F32),
        scratch_shapes=[pltpu.VMEM((TOK_POST, D_MODEL), BF16), pltpu.VMEM((TOK_POST, D_MODEL), F32)],
        compiler_params=_params(("parallel",)),
        name="post",
    )(x1, flat(yg), flat(of), flat(ob), g, w["hon"], w["wout"], w["n2"],
      w["wg2"], w["wu2"], w["wd2"], w["nf"])
    return out.reshape(batch, seq, D_MODEL)


def kernel(x_prompt, x_sample, ffn1_norm, ffn1_w_gate, ffn1_w_up, ffn1_w_down, mix_norm, w_in,
           gmlp_ln_g, gmlp_ln_b, gmlp_w_s, gmlp_b_s, gmlp_out_norm,
           hgrn_lb_fwd, hgrn_lb_bwd, hgrn_out_norm, w_out,
           ffn2_norm, ffn2_w_gate, ffn2_w_up, ffn2_w_down, final_norm):
    lb_of = lambda p: jnp.cumsum(jax.nn.softmax(p.astype(F32), axis=0), axis=0)[0]
    w = dict(
        lb_min=jnp.minimum(jnp.min(lb_of(hgrn_lb_fwd)), jnp.min(lb_of(hgrn_lb_bwd))),
        n1=_row(ffn1_norm[0]), wg1=ffn1_w_gate[0].astype(BF16), wu1=ffn1_w_up[0].astype(BF16),
        wd1=ffn1_w_down[0].astype(BF16), nmix=_row(mix_norm[0]), win=w_in[0].astype(BF16),
        lng=_row(gmlp_ln_g[0]), lnb=_row(gmlp_ln_b[0]),
        lbf=hgrn_lb_fwd.astype(F32), lbb=hgrn_lb_bwd.astype(F32),
        ws=gmlp_w_s[0].astype(BF16), bs=gmlp_b_s[0].T.astype(F32), gon=_row(gmlp_out_norm[0]),
        hon=_row(hgrn_out_norm[0]), wout=w_out[0].astype(BF16),
        n2=_row(ffn2_norm[0]), wg2=ffn2_w_gate[0].astype(BF16), wu2=ffn2_w_up[0].astype(BF16),
        wd2=ffn2_w_down[0].astype(BF16), nf=_row(final_norm),
    )
    return _trunk(x_prompt, w), _trunk(x_sample, w)
```

```python
import functools

import jax
import jax.numpy as jnp
import numpy as np
from jax import lax
from jax.experimental import pallas as pl
from jax.experimental.pallas import tpu as pltpu

F32 = jnp.float32
BF16 = jnp.bfloat16

D_MODEL = 1024
D_GMLP = 512
D_HGRN = 512
N_HEADS = 4
HEAD = 128
CHUNK = 128
HALF = CHUNK // 2
D_FF = 2816
D_IN = 2 * D_GMLP + 5 * D_HGRN
EPS = 1e-6

FF_COLS = 256
FF_STEPS = D_FF // FF_COLS
TOK_PRE = 512
TOK_POST = 512
TOK_MIX = 512
MIX_CHUNKS = TOK_MIX // CHUNK
N_LEVELS = 7
MAX_EXP_ARG = 80.0
VMEM_LIMIT = 56 * 1024 * 1024

SQRT_HALF = float(np.sqrt(0.5).astype(np.float32))

NT_DIMS = (((1,), (1,)), ((), ()))
TN_DIMS = (((0,), (0,)), ((), ()))


def _rms(x, gain):
    return x * lax.rsqrt(jnp.mean(x * x, axis=-1, keepdims=True) + EPS) * gain


def _silu(x):
    return x * jax.nn.sigmoid(x)


def _gelu(x):
    return 0.5 * x * (1.0 + lax.erf(x * SQRT_HALF))


def _swiglu_into(acc_ref, hb_ref, wg_ref, wu_ref, wd_ref):
    acc_ref[...] = jnp.zeros_like(acc_ref)
    for c in range(FF_STEPS):
        cols = slice(c * FF_COLS, (c + 1) * FF_COLS)
        g = jnp.dot(hb_ref[...], wg_ref[:, cols], preferred_element_type=F32)
        u = jnp.dot(hb_ref[...], wu_ref[:, cols], preferred_element_type=F32)
        a = (_silu(g) * u).astype(BF16)
        acc_ref[...] += jnp.dot(a, wd_ref[cols, :], preferred_element_type=F32)


def _lower_bound(p):
    e = jnp.exp(p - jnp.max(p, axis=0, keepdims=True))
    return e[0:1, :] / jnp.sum(e, axis=0, keepdims=True)


def _pre_kernel(x_ref, n1_ref, wg_ref, wu_ref, wd_ref, nmix_ref, win_ref, lng_ref, lnb_ref,
                lbf_ref, lbb_ref,
                x1_ref, gu_ref, gv_ref, qs_ref, i_ref, ff_ref, fb_ref, g_ref,
                hb_ref, acc_ref):
    x = x_ref[...]
    hb_ref[...] = _rms(x, n1_ref[...]).astype(BF16)
    _swiglu_into(acc_ref, hb_ref, wg_ref, wu_ref, wd_ref)
    x1 = x + 0.5 * acc_ref[...]
    x1_ref[...] = x1
    hb_ref[...] = _rms(x1, nmix_ref[...]).astype(BF16)

    def proj(j):
        return jnp.dot(hb_ref[...], win_ref[:, j * D_HGRN:(j + 1) * D_HGRN],
                       preferred_element_type=F32)

    gu_ref[...] = _gelu(proj(0)).astype(BF16)
    v = _gelu(proj(1))
    mu = jnp.mean(v, axis=-1, keepdims=True)
    vc = v - mu
    gv_ref[...] = (vc * lax.rsqrt(jnp.mean(vc * vc, axis=-1, keepdims=True) + EPS)
                   * lng_ref[...] + lnb_ref[...]).astype(BF16)
    qs_ref[...] = _silu(proj(2)).astype(BF16)
    i_ref[...] = proj(3).astype(BF16)
    lbf = _lower_bound(lbf_ref[...])
    ff_ref[...] = lbf + (1.0 - lbf) * jax.nn.sigmoid(proj(4))
    lbb = _lower_bound(lbb_ref[...])
    fb_ref[...] = lbb + (1.0 - lbb) * jax.nn.sigmoid(proj(5))
    g_ref[...] = proj(6).astype(BF16)


def _hgrn_chunk(qs, v, f, st_ref, dm_ref, forward, bounded):
    lf = jnp.log(f)
    k = 1.0 - f
    hi = lf.astype(BF16)
    lo = (lf - hi.astype(F32)).astype(BF16)
    hilo = jnp.concatenate([hi, lo], axis=0)
    cum = lambda i: jnp.dot(dm_ref[i], hilo, preferred_element_type=F32)
    row = lax.broadcasted_iota(jnp.int32, (CHUNK, CHUNK), 0)
    col = lax.broadcasted_iota(jnp.int32, (CHUNK, CHUNK), 1)
    heads = [slice(h * HEAD, (h + 1) * HEAD) for h in range(N_HEADS)]
    vb = v.astype(BF16)

    if not bounded:
        x = cum(0)
        sum_a = jnp.sum(lf[:HALF], axis=0, keepdims=True)
        sum_b = jnp.sum(lf[HALF:], axis=0, keepdims=True)
        r_in, r_out = (sum_a, sum_b) if forward else (sum_b, sum_a)
        qt = qs * jnp.exp(x)
        kt = k * jnp.exp(-x)
        qh = (qt * jnp.exp(r_in)).astype(BF16)
        kh = (kt * jnp.exp(r_out)).astype(BF16)
        dec = jnp.exp(r_in + r_out)
        qt = qt.astype(BF16)
        kt = kt.astype(BF16)
        visible = (col <= row) if forward else (col >= row)
        scores = [jnp.where(visible,
                            lax.dot_general(qt[:, sl], kt[:, sl], NT_DIMS, preferred_element_type=F32),
                            0.0) for sl in heads]
        same_pos = [None] * N_HEADS
    else:
        qh = (qs * jnp.exp(cum(N_LEVELS))).astype(BF16)
        kh = (k * jnp.exp(cum(N_LEVELS + 1))).astype(BF16)
        dec = jnp.exp(jnp.sum(lf, axis=0, keepdims=True))
        tok = lax.broadcasted_iota(jnp.int32, (CHUNK, D_HGRN), 0)
        scores = [jnp.zeros((CHUNK, CHUNK), F32)] * N_HEADS
        for lvl in range(N_LEVELS):
            m = 1 << lvl
            is_query = ((tok & m) != 0) if forward else ((tok & m) == 0)
            z = (jnp.where(is_query, qs, k) * jnp.exp(-jnp.abs(cum(lvl)))).astype(BF16)
            pair = (((row ^ col) >> lvl) == 1) & ((col < row) if forward else (col > row))
            scores = [sc + jnp.where(pair,
                                     lax.dot_general(z[:, sl], z[:, sl], NT_DIMS,
                                                     preferred_element_type=F32), 0.0)
                      for sc, sl in zip(scores, heads)]
        same_pos = [jnp.sum(qs[:, sl] * k[:, sl], axis=-1, keepdims=True) * v[:, sl] for sl in heads]

    outs = []
    for h, sl in enumerate(heads):
        st = st_ref[h]
        o = jnp.dot(scores[h].astype(BF16), vb[:, sl], preferred_element_type=F32)
        o += lax.dot_general(qh[:, sl], st.astype(BF16), NT_DIMS, preferred_element_type=F32)
        if same_pos[h] is not None:
            o += same_pos[h]
        outs.append(o)
        st_ref[h] = st * dec[:, sl] + lax.dot_general(vb[:, sl], kh[:, sl], TN_DIMS,
                                                      preferred_element_type=F32)
    return jnp.concatenate(outs, axis=-1)


def _mixer_kernel(gu_ref, gv_ref, qsf_ref, if_ref, ff_ref, qsb_ref, ib_ref, fb_ref,
                  ws_ref, bs_ref, gon_ref, dfw_ref, dbw_ref,
                  yg_ref, of_ref, ob_ref,
                  sf_ref, sb_ref, *, bounded):
    @pl.when(pl.program_id(1) == 0)
    def _():
        sf_ref[...] = jnp.zeros_like(sf_ref)
        sb_ref[...] = jnp.zeros_like(sb_ref)

    for c in range(MIX_CHUNKS):
        rows = slice(c * CHUNK, (c + 1) * CHUNK)
        gvb = gv_ref[rows, :]
        mixed = []
        for h in range(N_HEADS):
            sl = slice(h * HEAD, (h + 1) * HEAD)
            mixed.append(jnp.dot(ws_ref[h], gvb[:, sl], preferred_element_type=F32)
                         + bs_ref[:, h:h + 1])
        y = gu_ref[rows, :] * jnp.concatenate(mixed, axis=-1)
        yg_ref[rows, :] = _rms(y, gon_ref[...]).astype(BF16)
        of_ref[rows, :] = _hgrn_chunk(qsf_ref[rows, :], if_ref[rows, :], ff_ref[rows, :],
                                      sf_ref, dfw_ref, True, bounded)
        rrows = slice((MIX_CHUNKS - 1 - c) * CHUNK, (MIX_CHUNKS - c) * CHUNK)
        ob_ref[rrows, :] = _hgrn_chunk(qsb_ref[rrows, :], ib_ref[rrows, :], fb_ref[rrows, :],
                                       sb_ref, dbw_ref, False, bounded)


def _post_kernel(x1_ref, yg_ref, of_ref, ob_ref, g_ref, hon_ref, wout_ref, n2_ref,
                 wg_ref, wu_ref, wd_ref, nf_ref,
                 out_ref,
                 hb_ref, acc_ref):
    o = of_ref[...] + ob_ref[...]
    heads = []
    for h in range(N_HEADS):
        sl = slice(h * HEAD, (h + 1) * HEAD)
        heads.append(_rms(o[:, sl], hon_ref[:, sl]))
    yh = jnp.concatenate(heads, axis=-1) * _silu(g_ref[...].astype(F32))
    x2 = (x1_ref[...]
          + jnp.dot(yg_ref[...], wout_ref[:D_GMLP, :], preferred_element_type=F32)
          + jnp.dot(yh.astype(BF16), wout_ref[D_GMLP:, :], preferred_element_type=F32))
    hb_ref[...] = _rms(x2, n2_ref[...]).astype(BF16)
    _swiglu_into(acc_ref, hb_ref, wg_ref, wu_ref, wd_ref)
    x3 = x2 + 0.5 * acc_ref[...]
    out_ref[...] = _rms(x3, nf_ref[...])


def _resident(shape):
    nd = len(shape)
    return pl.BlockSpec(shape, lambda *_: (0,) * nd, pipeline_mode=pl.Buffered(1))


def _decay_matrices(bounded):
    t = np.arange(CHUNK)
    out = []
    for forward in (True, False):
        cums = (t[None, :] <= t[:, None]) if forward else (t[None, :] >= t[:, None])
        cums = cums.astype(np.float32)
        if bounded:
            mats = []
            for lvl in range(N_LEVELS):
                m = 1 << lvl
                ref = (t // (2 * m)) * (2 * m) + (m - 1 if forward else m)
                mats.append(cums - cums[ref])
            mats += [cums, 1.0 - cums]
        else:
            mats = [cums - cums[HALF - 1 if forward else HALF]]
        d = np.stack(mats)
        out.append(jnp.asarray(np.concatenate([d, d], axis=2), dtype=BF16))
    return out


def _row(v):
    return v.reshape(1, -1).astype(F32)


def _params(sem):
    return pltpu.CompilerParams(dimension_semantics=sem, vmem_limit_bytes=VMEM_LIMIT)


def _trunk(x, w):
    batch, seq, _ = x.shape
    n_tok = batch * seq
    assert n_tok % TOK_PRE == 0 and n_tok % TOK_POST == 0 and seq % TOK_MIX == 0
    xf = x.reshape(n_tok, D_MODEL)

    tok = lambda t, d: pl.BlockSpec((t, d), lambda i: (i, 0))
    half = lambda dt: jax.ShapeDtypeStruct((n_tok, D_HGRN), dt)

    pre_out = pl.pallas_call(
        _pre_kernel,
        grid=(n_tok // TOK_PRE,),
        in_specs=[tok(TOK_PRE, D_MODEL), _resident((1, D_MODEL)),
                  _resident((D_MODEL, D_FF)), _resident((D_MODEL, D_FF)), _resident((D_FF, D_MODEL)),
                  _resident((1, D_MODEL)), _resident((D_MODEL, D_IN)),
                  _resident((1, D_GMLP)), _resident((1, D_GMLP)),
                  _resident(w["lbf"].shape), _resident(w["lbb"].shape)],
        out_specs=[tok(TOK_PRE, D_MODEL)] + [tok(TOK_PRE, D_HGRN)] * 7,
        out_shape=[jax.ShapeDtypeStruct((n_tok, D_MODEL), F32)]
                  + [half(dt) for dt in (BF16, BF16, BF16, BF16, F32, F32, BF16)],
        scratch_shapes=[pltpu.VMEM((TOK_PRE, D_MODEL), BF16), pltpu.VMEM((TOK_PRE, D_MODEL), F32)],
        compiler_params=_params(("parallel",)),
        name="pre",
    )(xf, w["n1"], w["wg1"], w["wu1"], w["wd1"], w["nmix"], w["win"], w["lng"], w["lnb"],
      w["lbf"], w["lbb"])
    x1, gu, gv, qs, iv, ff, fb, g = pre_out

    n_blk = seq // TOK_MIX
    seq3 = lambda a: a.reshape(batch, seq, D_HGRN)
    fwd = pl.BlockSpec((None, TOK_MIX, D_HGRN), lambda b, c: (b, c, 0))
    bwd = pl.BlockSpec((None, TOK_MIX, D_HGRN), lambda b, c: (b, n_blk - 1 - c, 0))
    seq_shape = lambda dt: jax.ShapeDtypeStruct((batch, seq, D_HGRN), dt)

    def mixer(bounded):
        dfw, dbw = _decay_matrices(bounded)
        return pl.pallas_call(
            functools.partial(_mixer_kernel, bounded=bounded),
            grid=(batch, n_blk),
            in_specs=[fwd, fwd, fwd, fwd, fwd, bwd, bwd, bwd,
                      _resident(w["ws"].shape), _resident(w["bs"].shape), _resident((1, D_GMLP)),
                      _resident(dfw.shape), _resident(dbw.shape)],
            out_specs=[fwd, fwd, bwd],
            out_shape=[seq_shape(BF16), seq_shape(F32), seq_shape(F32)],
            scratch_shapes=[pltpu.VMEM((N_HEADS, HEAD, HEAD), F32),
                            pltpu.VMEM((N_HEADS, HEAD, HEAD), F32)],
            compiler_params=_params(("parallel", "arbitrary")),
            name="mixer_bounded" if bounded else "mixer",
        )(seq3(gu), seq3(gv), seq3(qs), seq3(iv), seq3(ff), seq3(qs), seq3(iv), seq3(fb),
          w["ws"], w["bs"], w["gon"], dfw, dbw)

    yg, of, ob = lax.cond(w["lb_min"] >= np.exp(-MAX_EXP_ARG / HALF),
                          lambda: mixer(False), lambda: mixer(True))

    flat = lambda a: a.reshape(n_tok, D_HGRN)
    out = pl.pallas_call(
        _post_kernel,
        grid=(n_tok // TOK_POST,),
        in_specs=[tok(TOK_POST, D_MODEL)] + [tok(TOK_POST, D_HGRN)] * 4
                 + [_resident((1, D_HGRN)), _resident((D_MODEL, D_MODEL)), _resident((1, D_MODEL)),
                    _resident((D_MODEL, D_FF)), _resident((D_MODEL, D_FF)), _resident((D_FF, D_MODEL)),
                    _resident((1, D_MODEL))],
        out_specs=tok(TOK_POST, D_MODEL),
        out_shape=jax.ShapeDtypeStruct((n_tok, D_MODEL), F32),
        scratch_shapes=[pltpu.VMEM((TOK_POST, D_MODEL), BF16), pltpu.VMEM((TOK_POST, D_MODEL), F32)],
        compiler_params=_params(("parallel",)),
        name="post",
    )(x1, flat(yg), flat(of), flat(ob), g, w["hon"], w["wout"], w["n2"],
      w["wg2"], w["wu2"], w["wd2"], w["nf"])
    return out.reshape(batch, seq, D_MODEL)


def kernel(x_prompt, x_sample, ffn1_norm, ffn1_w_gate, ffn1_w_up, ffn1_w_down, mix_norm, w_in,
           gmlp_ln_g, gmlp_ln_b, gmlp_w_s, gmlp_b_s, gmlp_out_norm,
           hgrn_lb_fwd, hgrn_lb_bwd, hgrn_out_norm, w_out,
           ffn2_norm, ffn2_w_gate, ffn2_w_up, ffn2_w_down, final_norm):
    lb_of = lambda p: jnp.cumsum(jax.nn.softmax(p.astype(F32), axis=0), axis=0)[0]
    w = dict(
        lb_min=jnp.minimum(jnp.min(lb_of(hgrn_lb_fwd)), jnp.min(lb_of(hgrn_lb_bwd))),
        n1=_row(ffn1_norm[0]), wg1=ffn1_w_gate[0].astype(BF16), wu1=ffn1_w_up[0].astype(BF16),
        wd1=ffn1_w_down[0].astype(BF16), nmix=_row(mix_norm[0]), win=w_in[0].astype(BF16),
        lng=_row(gmlp_ln_g[0]), lnb=_row(gmlp_ln_b[0]),
        lbf=hgrn_lb_fwd.astype(F32), lbb=hgrn_lb_bwd.astype(F32),
        ws=gmlp_w_s[0].astype(BF16), bs=gmlp_b_s[0].T.astype(F32), gon=_row(gmlp_out_norm[0]),
        hon=_row(hgrn_out_norm[0]), wout=w_out[0].astype(BF16),
        n2=_row(ffn2_norm[0]), wg2=ffn2_w_gate[0].astype(BF16), wu2=ffn2_w_up[0].astype(BF16),
        wd2=ffn2_w_down[0].astype(BF16), nf=_row(final_norm),
    )
    return _trunk(x_prompt, w), _trunk(x_sample, w)
```

```python
import functools

import jax
import jax.numpy as jnp
import numpy as np
from jax import lax
from jax.experimental import pallas as pl
from jax.experimental.pallas import tpu as pltpu

F32 = jnp.float32
BF16 = jnp.bfloat16

D_MODEL = 1024
D_GMLP = 512
D_HGRN = 512
N_HEADS = 4
HEAD = 128
CHUNK = 128
HALF = CHUNK // 2
D_FF = 2816
D_IN = 2 * D_GMLP + 5 * D_HGRN
EPS = 1e-6

FF_COLS = 256
FF_STEPS = D_FF // FF_COLS
TOK_PRE = 512
TOK_POST = 512
TOK_SCAN = 1024
PRE_CHUNKS = TOK_PRE // CHUNK
SCAN_CHUNKS = TOK_SCAN // CHUNK
LOCAL_STAGE_AFTER_FF_STEP = (1, 4, 7)
N_LEVELS = 7
MAX_EXP_ARG = 80.0
VMEM_LIMIT = 56 * 1024 * 1024

SQRT_HALF = float(np.sqrt(0.5).astype(np.float32))

NT_DIMS = (((1,), (1,)), ((), ()))
TN_DIMS = (((0,), (0,)), ((), ()))


def _rms(x, gain):
    return x * lax.rsqrt(jnp.mean(x * x, axis=-1, keepdims=True) + EPS) * gain


def _silu(x):
    return x * jax.nn.sigmoid(x)


def _gelu(x):
    return 0.5 * x * (1.0 + lax.erf(x * SQRT_HALF))


def _swiglu_into(acc_ref, hb_ref, wg_ref, wu_ref, wd_ref, between=None):
    acc_ref[...] = jnp.zeros_like(acc_ref)
    for c in range(FF_STEPS):
        cols = slice(c * FF_COLS, (c + 1) * FF_COLS)
        g = jnp.dot(hb_ref[...], wg_ref[:, cols], preferred_element_type=F32)
        u = jnp.dot(hb_ref[...], wu_ref[:, cols], preferred_element_type=F32)
        a = (_silu(g) * u).astype(BF16)
        acc_ref[...] += jnp.dot(a, wd_ref[cols, :], preferred_element_type=F32)
        if between is not None:
            between(c)


def _lower_bound(p):
    e = jnp.exp(p - jnp.max(p, axis=0, keepdims=True))
    return e[0:1, :] / jnp.sum(e, axis=0, keepdims=True)


def _heads():
    return [slice(h * HEAD, (h + 1) * HEAD) for h in range(N_HEADS)]


def _log_decay(f, dm_ref):
    lf = jnp.log(f)
    hi = lf.astype(BF16)
    lo = (lf - hi.astype(F32)).astype(BF16)
    hilo = jnp.concatenate([hi, lo], axis=0)
    return lf, 1.0 - f, lambda i: jnp.dot(dm_ref[i], hilo, preferred_element_type=F32)


def _single_ref_prep(f, dm_ref, forward):
    lf, k, cum = _log_decay(f, dm_ref)
    sum_a = jnp.sum(lf[:HALF], axis=0, keepdims=True)
    sum_b = jnp.sum(lf[HALF:], axis=0, keepdims=True)
    r_in, r_out = (sum_a, sum_b) if forward else (sum_b, sum_a)
    return k, cum(0), r_in, r_out


def _single_ref_scores(qs, vb, prep, forward):
    k, x, r_in, r_out = prep
    row = lax.broadcasted_iota(jnp.int32, (CHUNK, CHUNK), 0)
    col = lax.broadcasted_iota(jnp.int32, (CHUNK, CHUNK), 1)
    qt = qs * jnp.exp(x)
    kt = k * jnp.exp(-x)
    qh = (qt * jnp.exp(r_in)).astype(BF16)
    kh = (kt * jnp.exp(r_out)).astype(BF16)
    dec = jnp.exp(r_in + r_out)
    qt = qt.astype(BF16)
    kt = kt.astype(BF16)
    visible = (col <= row) if forward else (col >= row)
    scores = [jnp.where(visible,
                        lax.dot_general(qt[:, sl], kt[:, sl], NT_DIMS, preferred_element_type=F32), 0.0)
              for sl in _heads()]
    ds = [lax.dot_general(vb[:, sl], kh[:, sl], TN_DIMS, preferred_element_type=F32) for sl in _heads()]
    return scores, qh, ds, dec


def _bounded_scores(qs, v, f, dm_ref, forward):
    lf, k, cum = _log_decay(f, dm_ref)
    row = lax.broadcasted_iota(jnp.int32, (CHUNK, CHUNK), 0)
    col = lax.broadcasted_iota(jnp.int32, (CHUNK, CHUNK), 1)
    tok = lax.broadcasted_iota(jnp.int32, (CHUNK, D_HGRN), 0)
    qh = (qs * jnp.exp(cum(N_LEVELS))).astype(BF16)
    kh = (k * jnp.exp(cum(N_LEVELS + 1))).astype(BF16)
    dec = jnp.exp(jnp.sum(lf, axis=0, keepdims=True))
    scores = [jnp.zeros((CHUNK, CHUNK), F32)] * N_HEADS
    for lvl in range(N_LEVELS):
        m = 1 << lvl
        is_query = ((tok & m) != 0) if forward else ((tok & m) == 0)
        z = (jnp.where(is_query, qs, k) * jnp.exp(-jnp.abs(cum(lvl)))).astype(BF16)
        pair = (((row ^ col) >> lvl) == 1) & ((col < row) if forward else (col > row))
        scores = [sc + jnp.where(pair, lax.dot_general(z[:, sl], z[:, sl], NT_DIMS,
                                                       preferred_element_type=F32), 0.0)
                  for sc, sl in zip(scores, _heads())]
    same_pos = jnp.concatenate(
        [jnp.sum(qs[:, sl] * k[:, sl], axis=-1, keepdims=True) * v[:, sl] for sl in _heads()], axis=-1)
    vb = v.astype(BF16)
    ds = [lax.dot_general(vb[:, sl], kh[:, sl], TN_DIMS, preferred_element_type=F32) for sl in _heads()]
    return scores, qh, ds, dec, same_pos


def _local_work(act, consts, outs, bounded):
    gu_s, gv_s, qs_s, i_s, ff_s, fb_s = act
    ws_ref, bs_ref, gon_ref, dfw_ref, dbw_ref = consts
    yg_ref, oi_ref, qhf_ref, qhb_ref, dsf_ref, dsb_ref, decf_ref, decb_ref = outs
    chunks = [slice(c * CHUNK, (c + 1) * CHUNK) for c in range(PRE_CHUNKS)]
    dirs = ((ff_s, dfw_ref, True, qhf_ref, dsf_ref, decf_ref),
            (fb_s, dbw_ref, False, qhb_ref, dsb_ref, decb_ref))

    def put(c, rows, qh, ds, dec, qh_ref, ds_ref, dec_ref):
        qh_ref[rows, :] = qh
        dec_ref[c] = dec
        for h in range(N_HEADS):
            ds_ref[c, h] = ds[h]

    prep = []
    for c, rows in enumerate(chunks):
        gv = gv_s[rows, :]
        mixed = [jnp.dot(ws_ref[h], gv[:, sl], preferred_element_type=F32) + bs_ref[:, h:h + 1]
                 for h, sl in enumerate(_heads())]
        yg_ref[rows, :] = _rms(gu_s[rows, :] * jnp.concatenate(mixed, axis=-1),
                               gon_ref[...]).astype(BF16)
        if not bounded:
            prep.append([_single_ref_prep(f_s[rows, :], dm_ref, fw) for f_s, dm_ref, fw, *_ in dirs])
        yield

    scores = []
    for c, rows in enumerate(chunks):
        qs, v = qs_s[rows, :], i_s[rows, :]
        both, extra = [], None
        for d, (f_s, dm_ref, fw, *out_refs) in enumerate(dirs):
            if bounded:
                sc, qh, ds, dec, same = _bounded_scores(qs, v, f_s[rows, :], dm_ref, fw)
                extra = same if extra is None else extra + same
            else:
                sc, qh, ds, dec = _single_ref_scores(qs, v, prep[c][d], fw)
            put(c, rows, qh, ds, dec, *out_refs)
            both.append(sc)
        scores.append(([a + b for a, b in zip(*both)], extra))
        yield

    for c, rows in enumerate(chunks):
        v = i_s[rows, :]
        sc, extra = scores[c]
        intra = jnp.concatenate([jnp.dot(s.astype(BF16), v[:, sl], preferred_element_type=F32)
                                 for s, sl in zip(sc, _heads())], axis=-1)
        oi_ref[rows, :] = intra if extra is None else intra + extra
        yield


def _pre_kernel(x_ref, n1_ref, wg_ref, wu_ref, wd_ref, nmix_ref, win_ref, lng_ref, lnb_ref,
                lbf_ref, lbb_ref, ws_ref, bs_ref, gon_ref, dfw_ref, dbw_ref,
                x1_ref, g_ref, yg_ref, oi_ref, qhf_ref, qhb_ref, dsf_ref, dsb_ref, decf_ref, decb_ref,
                hb_ref, acc_ref, gu_s, gv_s, qs_s, i_s, ff_s, fb_s, *, bounded):
    act = (gu_s, gv_s, qs_s, i_s, ff_s, fb_s)

    @pl.when(pl.program_id(0) == 0)
    def _():
        for r in (gu_s, gv_s, qs_s, i_s):
            r[...] = jnp.zeros_like(r)
        ff_s[...] = jnp.ones_like(ff_s)
        fb_s[...] = jnp.ones_like(fb_s)

    local = _local_work(act, (ws_ref, bs_ref, gon_ref, dfw_ref, dbw_ref),
                        (yg_ref, oi_ref, qhf_ref, qhb_ref, dsf_ref, dsb_ref, decf_ref, decb_ref), bounded)

    def between(c):
        next(local, None)

    x = x_ref[...]
    hb_ref[...] = _rms(x, n1_ref[...]).astype(BF16)
    _swiglu_into(acc_ref, hb_ref, wg_ref, wu_ref, wd_ref, between)
    x1 = x + 0.5 * acc_ref[...]
    x1_ref[...] = x1
    hb_ref[...] = _rms(x1, nmix_ref[...]).astype(BF16)

    def proj(j):
        return jnp.dot(hb_ref[...], win_ref[:, j * D_HGRN:(j + 1) * D_HGRN],
                       preferred_element_type=F32)

    gu_s[...] = _gelu(proj(0)).astype(BF16)
    for _ in local:
        pass
    v = _gelu(proj(1))
    mu = jnp.mean(v, axis=-1, keepdims=True)
    vc = v - mu
    gv_s[...] = (vc * lax.rsqrt(jnp.mean(vc * vc, axis=-1, keepdims=True) + EPS)
                 * lng_ref[...] + lnb_ref[...]).astype(BF16)
    qs_s[...] = _silu(proj(2)).astype(BF16)
    i_s[...] = proj(3).astype(BF16)
    lbf = _lower_bound(lbf_ref[...])
    ff_s[...] = lbf + (1.0 - lbf) * jax.nn.sigmoid(proj(4))
    lbb = _lower_bound(lbb_ref[...])
    fb_s[...] = lbb + (1.0 - lbb) * jax.nn.sigmoid(proj(5))
    g_ref[...] = proj(6).astype(BF16)


def _scan_kernel(qhf_ref, dsf_ref, decf_ref, qhb_ref, dsb_ref, decb_ref,
                 of_ref, ob_ref,
                 sf_ref, sb_ref):
    @pl.when(pl.program_id(1) == 0)
    def _():
        sf_ref[...] = jnp.zeros_like(sf_ref)
        sb_ref[...] = jnp.zeros_like(sb_ref)

    for c in range(SCAN_CHUNKS):
        for cc, qh_ref, ds_ref, dec_ref, o_ref, st_ref in (
                (c, qhf_ref, dsf_ref, decf_ref, of_ref, sf_ref),
                (SCAN_CHUNKS - 1 - c, qhb_ref, dsb_ref, decb_ref, ob_ref, sb_ref)):
            rows = slice(cc * CHUNK, (cc + 1) * CHUNK)
            outs = []
            for h in range(N_HEADS):
                sl = slice(h * HEAD, (h + 1) * HEAD)
                st = st_ref[h]
                outs.append(lax.dot_general(qh_ref[rows, sl], st.astype(BF16), NT_DIMS,
                                            preferred_element_type=F32))
                st_ref[h] = st * dec_ref[cc][:, sl] + ds_ref[cc, h]
            o_ref[rows, :] = jnp.concatenate(outs, axis=-1)


def _post_kernel(x1_ref, yg_ref, oi_ref, of_ref, ob_ref, g_ref, hon_ref, wout_ref, n2_ref,
                 wg_ref, wu_ref, wd_ref, nf_ref,
                 out_ref,
                 hb_ref, acc_ref):
    o = oi_ref[...] + of_ref[...] + ob_ref[...]
    heads = []
    for h in range(N_HEADS):
        sl = slice(h * HEAD, (h + 1) * HEAD)
        heads.append(_rms(o[:, sl], hon_ref[:, sl]))
    yh = jnp.concatenate(heads, axis=-1) * _silu(g_ref[...].astype(F32))
    x2 = (x1_ref[...]
          + jnp.dot(yg_ref[...], wout_ref[:D_GMLP, :], preferred_element_type=F32)
          + jnp.dot(yh.astype(BF16), wout_ref[D_GMLP:, :], preferred_element_type=F32))
    hb_ref[...] = _rms(x2, n2_ref[...]).astype(BF16)
    _swiglu_into(acc_ref, hb_ref, wg_ref, wu_ref, wd_ref)
    x3 = x2 + 0.5 * acc_ref[...]
    out_ref[...] = _rms(x3, nf_ref[...])


def _resident(shape):
    nd = len(shape)
    return pl.BlockSpec(shape, lambda *_: (0,) * nd, pipeline_mode=pl.Buffered(1))


def _decay_matrices(bounded):
    t = np.arange(CHUNK)
    out = []
    for forward in (True, False):
        cums = (t[None, :] <= t[:, None]) if forward else (t[None, :] >= t[:, None])
        cums = cums.astype(np.float32)
        if bounded:
            mats = []
            for lvl in range(N_LEVELS):
                m = 1 << lvl
                ref = (t // (2 * m)) * (2 * m) + (m - 1 if forward else m)
                mats.append(cums - cums[ref])
            mats += [cums, 1.0 - cums]
        else:
            mats = [cums - cums[HALF - 1 if forward else HALF]]
        d = np.stack(mats)
        out.append(jnp.asarray(np.concatenate([d, d], axis=2), dtype=BF16))
    return out


def _row(v):
    return v.reshape(1, -1).astype(F32)


def _params(sem):
    return pltpu.CompilerParams(dimension_semantics=sem, vmem_limit_bytes=VMEM_LIMIT)


def _trunk(x, w):
    batch, seq, _ = x.shape
    n_tok = batch * seq
    assert n_tok % TOK_PRE == 0 and n_tok % TOK_POST == 0 and seq % TOK_SCAN == 0
    xf = x.reshape(n_tok, D_MODEL)
    n_pre = n_tok // TOK_PRE
    n_chunks = n_tok // CHUNK

    cur = lambda d: pl.BlockSpec((TOK_PRE, d), lambda i: (jnp.minimum(i, n_pre - 1), 0))
    prev = lambda d: pl.BlockSpec((TOK_PRE, d), lambda i: (jnp.maximum(i - 1, 0), 0))
    prev_ds = pl.BlockSpec((PRE_CHUNKS, N_HEADS, HEAD, HEAD),
                           lambda i: (jnp.maximum(i - 1, 0), 0, 0, 0))
    prev_dec = pl.BlockSpec((PRE_CHUNKS, 1, D_HGRN), lambda i: (jnp.maximum(i - 1, 0), 0, 0))
    tok_shape = lambda d, dt: jax.ShapeDtypeStruct((n_tok, d), dt)
    ds_shape = jax.ShapeDtypeStruct((n_chunks, N_HEADS, HEAD, HEAD), F32)
    dec_shape = jax.ShapeDtypeStruct((n_chunks, 1, D_HGRN), F32)
    act = lambda dt: pltpu.VMEM((TOK_PRE, D_HGRN), dt)

    def pre(bounded):
        dfw, dbw = _decay_matrices(bounded)
        return pl.pallas_call(
            functools.partial(_pre_kernel, bounded=bounded),
            grid=(n_pre + 1,),
            in_specs=[cur(D_MODEL), _resident((1, D_MODEL)),
                      _resident((D_MODEL, D_FF)), _resident((D_MODEL, D_FF)), _resident((D_FF, D_MODEL)),
                      _resident((1, D_MODEL)), _resident((D_MODEL, D_IN)),
                      _resident((1, D_GMLP)), _resident((1, D_GMLP)),
                      _resident(w["lbf"].shape), _resident(w["lbb"].shape),
                      _resident(w["ws"].shape), _resident(w["bs"].shape), _resident((1, D_GMLP)),
                      _resident(dfw.shape), _resident(dbw.shape)],
            out_specs=[cur(D_MODEL), cur(D_HGRN), prev(D_HGRN), prev(D_HGRN), prev(D_HGRN), prev(D_HGRN),
                       prev_ds, prev_ds, prev_dec, prev_dec],
            out_shape=[tok_shape(D_MODEL, F32), tok_shape(D_HGRN, BF16), tok_shape(D_HGRN, BF16),
                       tok_shape(D_HGRN, F32), tok_shape(D_HGRN, BF16), tok_shape(D_HGRN, BF16),
                       ds_shape, ds_shape, dec_shape, dec_shape],
            scratch_shapes=[pltpu.VMEM((TOK_PRE, D_MODEL), BF16), pltpu.VMEM((TOK_PRE, D_MODEL), F32),
                            act(BF16), act(BF16), act(BF16), act(BF16), act(F32), act(F32)],
            compiler_params=_params(("arbitrary",)),
            name="pre_bounded" if bounded else "pre",
        )(xf, w["n1"], w["wg1"], w["wu1"], w["wd1"], w["nmix"], w["win"], w["lng"], w["lnb"],
          w["lbf"], w["lbb"], w["ws"], w["bs"], w["gon"], dfw, dbw)

    x1, g, yg, oi, qhf, qhb, dsf, dsb, decf, decb = lax.cond(
        w["lb_min"] >= np.exp(-MAX_EXP_ARG / HALF), lambda: pre(False), lambda: pre(True))

    n_blk = seq // TOK_SCAN
    seq_chunks = seq // CHUNK
    fwd = lambda blk: pl.BlockSpec((None,) + blk, lambda b, c: (b, c) + (0,) * (len(blk) - 1))
    bwd = lambda blk: pl.BlockSpec((None,) + blk, lambda b, c: (b, n_blk - 1 - c) + (0,) * (len(blk) - 1))
    tok_blk, ds_blk, dec_blk = ((TOK_SCAN, D_HGRN), (SCAN_CHUNKS, N_HEADS, HEAD, HEAD),
                                (SCAN_CHUNKS, 1, D_HGRN))
    seq_out = jax.ShapeDtypeStruct((batch, seq, D_HGRN), F32)
    of, ob = pl.pallas_call(
        _scan_kernel,
        grid=(batch, n_blk),
        in_specs=[fwd(tok_blk), fwd(ds_blk), fwd(dec_blk), bwd(tok_blk), bwd(ds_blk), bwd(dec_blk)],
        out_specs=[fwd(tok_blk), bwd(tok_blk)],
        out_shape=[seq_out, seq_out],
        scratch_shapes=[pltpu.VMEM((N_HEADS, HEAD, HEAD), F32), pltpu.VMEM((N_HEADS, HEAD, HEAD), F32)],
        compiler_params=_params(("parallel", "arbitrary")),
        name="scan",
    )(qhf.reshape(batch, seq, D_HGRN), dsf.reshape(batch, seq_chunks, N_HEADS, HEAD, HEAD),
      decf.reshape(batch, seq_chunks, 1, D_HGRN),
      qhb.reshape(batch, seq, D_HGRN), dsb.reshape(batch, seq_chunks, N_HEADS, HEAD, HEAD),
      decb.reshape(batch, seq_chunks, 1, D_HGRN))

    tok = lambda d: pl.BlockSpec((TOK_POST, d), lambda i: (i, 0))
    flat = lambda a: a.reshape(n_tok, D_HGRN)
    out = pl.pallas_call(
        _post_kernel,
        grid=(n_tok // TOK_POST,),
        in_specs=[tok(D_MODEL)] + [tok(D_HGRN)] * 5
                 + [_resident((1, D_HGRN)), _resident((D_MODEL, D_MODEL)), _resident((1, D_MODEL)),
                    _resident((D_MODEL, D_FF)), _resident((D_MODEL, D_FF)), _resident((D_FF, D_MODEL)),
                    _resident((1, D_MODEL))],
        out_specs=tok(D_MODEL),
        out_shape=jax.ShapeDtypeStruct((n_tok, D_MODEL), F32),
        scratch_shapes=[pltpu.VMEM((TOK_POST, D_MODEL), BF16), pltpu.VMEM((TOK_POST, D_MODEL), F32)],
        compiler_params=_params(("parallel",)),
        name="post",
    )(x1, yg, oi, flat(of), flat(ob), g, w["hon"], w["wout"], w["n2"],
      w["wg2"], w["wu2"], w["wd2"], w["nf"])
    return out.reshape(batch, seq, D_MODEL)


def kernel(x_prompt, x_sample, ffn1_norm, ffn1_w_gate, ffn1_w_up, ffn1_w_down, mix_norm, w_in,
           gmlp_ln_g, gmlp_ln_b, gmlp_w_s, gmlp_b_s, gmlp_out_norm,
           hgrn_lb_fwd, hgrn_lb_bwd, hgrn_out_norm, w_out,
           ffn2_norm, ffn2_w_gate, ffn2_w_up, ffn2_w_down, final_norm):
    lb_of = lambda p: jnp.cumsum(jax.nn.softmax(p.astype(F32), axis=0), axis=0)[0]
    w = dict(
        lb_min=jnp.minimum(jnp.min(lb_of(hgrn_lb_fwd)), jnp.min(lb_of(hgrn_lb_bwd))),
        n1=_row(ffn1_norm[0]), wg1=ffn1_w_gate[0].astype(BF16), wu1=ffn1_w_up[0].astype(BF16),
        wd1=ffn1_w_down[0].astype(BF16), nmix=_row(mix_norm[0]), win=w_in[0].astype(BF16),
        lng=_row(gmlp_ln_g[0]), lnb=_row(gmlp_ln_b[0]),
        lbf=hgrn_lb_fwd.astype(F32), lbb=hgrn_lb_bwd.astype(F32),
        ws=gmlp_w_s[0].astype(BF16), bs=gmlp_b_s[0].T.astype(F32), gon=_row(gmlp_out_norm[0]),
        hon=_row(hgrn_out_norm[0]), wout=w_out[0].astype(BF16),
        n2=_row(ffn2_norm[0]), wg2=ffn2_w_gate[0].astype(BF16), wu2=ffn2_w_up[0].astype(BF16),
        wd2=ffn2_w_down[0].astype(BF16), nf=_row(final_norm),
    )
    return _trunk(x_prompt, w), _trunk(x_sample, w)
```

```python
import functools

import jax
import jax.numpy as jnp
import numpy as np
from jax import lax
from jax.experimental import pallas as pl
from jax.experimental.pallas import tpu as pltpu

F32 = jnp.float32
BF16 = jnp.bfloat16

D_MODEL = 1024
D_GMLP = 512
D_HGRN = 512
N_HEADS = 4
HEAD = 128
CHUNK = 128
HALF = CHUNK // 2
D_FF = 2816
D_IN = 2 * D_GMLP + 5 * D_HGRN
EPS = 1e-6

FF_COLS = 256
FF_STEPS = D_FF // FF_COLS
TOK_PRE = 512
TOK_POST = 512
TOK_MIX = 512
MIX_CHUNKS = TOK_MIX // CHUNK
MIX_SEQS = 2
N_LEVELS = 7
MAX_EXP_ARG = 80.0
VMEM_LIMIT = 56 * 1024 * 1024

SQRT_HALF = float(np.sqrt(0.5).astype(np.float32))

NT_DIMS = (((1,), (1,)), ((), ()))
TN_DIMS = (((0,), (0,)), ((), ()))


def _rms(x, gain):
    return x * lax.rsqrt(jnp.mean(x * x, axis=-1, keepdims=True) + EPS) * gain


def _silu(x):
    return x * jax.nn.sigmoid(x)


def _gelu(x):
    return 0.5 * x * (1.0 + lax.erf(x * SQRT_HALF))


def _swiglu_into(acc_ref, hb_ref, wg_ref, wu_ref, wd_ref):
    acc_ref[...] = jnp.zeros_like(acc_ref)
    for c in range(FF_STEPS):
        cols = slice(c * FF_COLS, (c + 1) * FF_COLS)
        g = jnp.dot(hb_ref[...], wg_ref[:, cols], preferred_element_type=F32)
        u = jnp.dot(hb_ref[...], wu_ref[:, cols], preferred_element_type=F32)
        a = (_silu(g) * u).astype(BF16)
        acc_ref[...] += jnp.dot(a, wd_ref[cols, :], preferred_element_type=F32)


def _lower_bound(p):
    e = jnp.exp(p - jnp.max(p, axis=0, keepdims=True))
    return e[0:1, :] / jnp.sum(e, axis=0, keepdims=True)


def _pre_kernel(x_ref, n1_ref, wg_ref, wu_ref, wd_ref, nmix_ref, win_ref, lng_ref, lnb_ref,
                lbf_ref, lbb_ref,
                x1_ref, gu_ref, gv_ref, qs_ref, i_ref, ff_ref, fb_ref, g_ref,
                hb_ref, acc_ref):
    x = x_ref[...]
    hb_ref[...] = _rms(x, n1_ref[...]).astype(BF16)
    _swiglu_into(acc_ref, hb_ref, wg_ref, wu_ref, wd_ref)
    x1 = x + 0.5 * acc_ref[...]
    x1_ref[...] = x1
    hb_ref[...] = _rms(x1, nmix_ref[...]).astype(BF16)

    def proj(j):
        return jnp.dot(hb_ref[...], win_ref[:, j * D_HGRN:(j + 1) * D_HGRN],
                       preferred_element_type=F32)

    gu_ref[...] = _gelu(proj(0)).astype(BF16)
    v = _gelu(proj(1))
    mu = jnp.mean(v, axis=-1, keepdims=True)
    vc = v - mu
    gv_ref[...] = (vc * lax.rsqrt(jnp.mean(vc * vc, axis=-1, keepdims=True) + EPS)
                   * lng_ref[...] + lnb_ref[...]).astype(BF16)
    qs_ref[...] = _silu(proj(2)).astype(BF16)
    i_ref[...] = proj(3).astype(BF16)
    lbf = _lower_bound(lbf_ref[...])
    ff_ref[...] = lbf + (1.0 - lbf) * jax.nn.sigmoid(proj(4))
    lbb = _lower_bound(lbb_ref[...])
    fb_ref[...] = lbb + (1.0 - lbb) * jax.nn.sigmoid(proj(5))
    g_ref[...] = proj(6).astype(BF16)


def _heads():
    return [slice(h * HEAD, (h + 1) * HEAD) for h in range(N_HEADS)]


def _log_decay(f, dm_ref):
    lf = jnp.log(f)
    hi = lf.astype(BF16)
    lo = (lf - hi.astype(F32)).astype(BF16)
    hilo = jnp.concatenate([hi, lo], axis=0)
    return lf, 1.0 - f, lambda i: jnp.dot(dm_ref[i], hilo, preferred_element_type=F32)


def _single_ref_prep(f, dm_ref, forward):
    lf, k, cum = _log_decay(f, dm_ref)
    sum_a = jnp.sum(lf[:HALF], axis=0, keepdims=True)
    sum_b = jnp.sum(lf[HALF:], axis=0, keepdims=True)
    r_in, r_out = (sum_a, sum_b) if forward else (sum_b, sum_a)
    return k, cum(0), r_in, r_out


def _single_ref_scores(qs, vb, prep, forward):
    k, x, r_in, r_out = prep
    row = lax.broadcasted_iota(jnp.int32, (CHUNK, CHUNK), 0)
    col = lax.broadcasted_iota(jnp.int32, (CHUNK, CHUNK), 1)
    qt = qs * jnp.exp(x)
    kt = k * jnp.exp(-x)
    qh = (qt * jnp.exp(r_in)).astype(BF16)
    kh = (kt * jnp.exp(r_out)).astype(BF16)
    dec = jnp.exp(r_in + r_out)
    qt = qt.astype(BF16)
    kt = kt.astype(BF16)
    visible = (col <= row) if forward else (col >= row)
    scores = [jnp.where(visible,
                        lax.dot_general(qt[:, sl], kt[:, sl], NT_DIMS, preferred_element_type=F32), 0.0)
              for sl in _heads()]
    ds = [lax.dot_general(vb[:, sl], kh[:, sl], TN_DIMS, preferred_element_type=F32) for sl in _heads()]
    return scores, qh, ds, dec, None


def _bounded_scores(qs, v, f, dm_ref, forward):
    lf, k, cum = _log_decay(f, dm_ref)
    row = lax.broadcasted_iota(jnp.int32, (CHUNK, CHUNK), 0)
    col = lax.broadcasted_iota(jnp.int32, (CHUNK, CHUNK), 1)
    tok = lax.broadcasted_iota(jnp.int32, (CHUNK, D_HGRN), 0)
    qh = (qs * jnp.exp(cum(N_LEVELS))).astype(BF16)
    kh = (k * jnp.exp(cum(N_LEVELS + 1))).astype(BF16)
    dec = jnp.exp(jnp.sum(lf, axis=0, keepdims=True))
    scores = [jnp.zeros((CHUNK, CHUNK), F32)] * N_HEADS
    for lvl in range(N_LEVELS):
        m = 1 << lvl
        is_query = ((tok & m) != 0) if forward else ((tok & m) == 0)
        z = (jnp.where(is_query, qs, k) * jnp.exp(-jnp.abs(cum(lvl)))).astype(BF16)
        pair = (((row ^ col) >> lvl) == 1) & ((col < row) if forward else (col > row))
        scores = [sc + jnp.where(pair, lax.dot_general(z[:, sl], z[:, sl], NT_DIMS,
                                                       preferred_element_type=F32), 0.0)
                  for sc, sl in zip(scores, _heads())]
    same_pos = jnp.concatenate(
        [jnp.sum(qs[:, sl] * k[:, sl], axis=-1, keepdims=True) * v[:, sl] for sl in _heads()], axis=-1)
    vb = v.astype(BF16)
    ds = [lax.dot_general(vb[:, sl], kh[:, sl], TN_DIMS, preferred_element_type=F32) for sl in _heads()]
    return scores, qh, ds, dec, same_pos


def _mixer_kernel(gu_ref, gv_ref, qsf_ref, if_ref, ff_ref, qsb_ref, ib_ref, fb_ref,
                  ws_ref, bs_ref, gon_ref, dfw_ref, dbw_ref,
                  yg_ref, of_ref, ob_ref,
                  sf_ref, sb_ref, *, bounded):
    @pl.when(pl.program_id(1) == 0)
    def _():
        sf_ref[...] = jnp.zeros_like(sf_ref)
        sb_ref[...] = jnp.zeros_like(sb_ref)

    chunk_rows = [slice(c * CHUNK, (c + 1) * CHUNK) for c in range(MIX_CHUNKS)]
    streams = ((qsf_ref, if_ref, ff_ref, dfw_ref, True, of_ref, sf_ref, chunk_rows),
               (qsb_ref, ib_ref, fb_ref, dbw_ref, False, ob_ref, sb_ref, chunk_rows[::-1]))
    items = [(s, n) for s in range(MIX_SEQS) for n in range(MIX_CHUNKS)]

    prep = {}
    for s, n in items:
        rows = chunk_rows[n]
        gv = gv_ref[s, rows, :]
        mixed = [jnp.dot(ws_ref[h], gv[:, sl], preferred_element_type=F32) + bs_ref[:, h:h + 1]
                 for h, sl in enumerate(_heads())]
        yg_ref[s, rows, :] = _rms(gu_ref[s, rows, :] * jnp.concatenate(mixed, axis=-1),
                                  gon_ref[...]).astype(BF16)
        if not bounded:
            for d, (_, _, f_ref, dm_ref, fw, _, _, order) in enumerate(streams):
                prep[s, n, d] = _single_ref_prep(f_ref[s, order[n], :], dm_ref, fw)

    mid = {}
    for s, n in items:
        for d, (qs_ref, v_ref, f_ref, dm_ref, fw, _, _, order) in enumerate(streams):
            rows = order[n]
            if bounded:
                mid[s, n, d] = _bounded_scores(qs_ref[s, rows, :], v_ref[s, rows, :], f_ref[s, rows, :],
                                               dm_ref, fw)
            else:
                mid[s, n, d] = _single_ref_scores(qs_ref[s, rows, :], v_ref[s, rows, :],
                                                  prep[s, n, d], fw)

    state = {(s, d): [st_ref[s, h] for h in range(N_HEADS)]
             for s in range(MIX_SEQS) for d, (*_, st_ref, _) in enumerate(streams)}
    for n in range(MIX_CHUNKS):
        for s in range(MIX_SEQS):
            for d, (_, v_ref, _, _, _, o_ref, _, order) in enumerate(streams):
                rows = order[n]
                scores, qh, ds, dec, same_pos = mid[s, n, d]
                vb = v_ref[s, rows, :]
                outs = []
                for h, sl in enumerate(_heads()):
                    st = state[s, d][h]
                    o = jnp.dot(scores[h].astype(BF16), vb[:, sl], preferred_element_type=F32)
                    o += lax.dot_general(qh[:, sl], st.astype(BF16), NT_DIMS,
                                         preferred_element_type=F32)
                    outs.append(o)
                    state[s, d][h] = st * dec[:, sl] + ds[h]
                o = jnp.concatenate(outs, axis=-1)
                o_ref[s, rows, :] = o if same_pos is None else o + same_pos
    for (s, d), st in state.items():
        for h in range(N_HEADS):
            streams[d][6][s, h] = st[h]


def _post_kernel(x1_ref, yg_ref, of_ref, ob_ref, g_ref, hon_ref, wout_ref, n2_ref,
                 wg_ref, wu_ref, wd_ref, nf_ref,
                 out_ref,
                 hb_ref, acc_ref):
    o = of_ref[...] + ob_ref[...]
    heads = []
    for h in range(N_HEADS):
        sl = slice(h * HEAD, (h + 1) * HEAD)
        heads.append(_rms(o[:, sl], hon_ref[:, sl]))
    yh = jnp.concatenate(heads, axis=-1) * _silu(g_ref[...].astype(F32))
    x2 = (x1_ref[...]
          + jnp.dot(yg_ref[...], wout_ref[:D_GMLP, :], preferred_element_type=F32)
          + jnp.dot(yh.astype(BF16), wout_ref[D_GMLP:, :], preferred_element_type=F32))
    hb_ref[...] = _rms(x2, n2_ref[...]).astype(BF16)
    _swiglu_into(acc_ref, hb_ref, wg_ref, wu_ref, wd_ref)
    x3 = x2 + 0.5 * acc_ref[...]
    out_ref[...] = _rms(x3, nf_ref[...])


def _resident(shape):
    nd = len(shape)
    return pl.BlockSpec(shape, lambda *_: (0,) * nd, pipeline_mode=pl.Buffered(1))


def _decay_matrices(bounded):
    t = np.arange(CHUNK)
    out = []
    for forward in (True, False):
        cums = (t[None, :] <= t[:, None]) if forward else (t[None, :] >= t[:, None])
        cums = cums.astype(np.float32)
        if bounded:
            mats = []
            for lvl in range(N_LEVELS):
                m = 1 << lvl
                ref = (t // (2 * m)) * (2 * m) + (m - 1 if forward else m)
                mats.append(cums - cums[ref])
            mats += [cums, 1.0 - cums]
        else:
            mats = [cums - cums[HALF - 1 if forward else HALF]]
        d = np.stack(mats)
        out.append(jnp.asarray(np.concatenate([d, d], axis=2), dtype=BF16))
    return out


def _row(v):
    return v.reshape(1, -1).astype(F32)


def _params(sem):
    return pltpu.CompilerParams(dimension_semantics=sem, vmem_limit_bytes=VMEM_LIMIT)


def _trunk(x, w):
    batch, seq, _ = x.shape
    n_tok = batch * seq
    assert n_tok % TOK_PRE == 0 and n_tok % TOK_POST == 0 and seq % TOK_MIX == 0
    assert batch % MIX_SEQS == 0
    xf = x.reshape(n_tok, D_MODEL)

    tok = lambda t, d: pl.BlockSpec((t, d), lambda i: (i, 0))
    half = lambda dt: jax.ShapeDtypeStruct((n_tok, D_HGRN), dt)

    pre_out = pl.pallas_call(
        _pre_kernel,
        grid=(n_tok // TOK_PRE,),
        in_specs=[tok(TOK_PRE, D_MODEL), _resident((1, D_MODEL)),
                  _resident((D_MODEL, D_FF)), _resident((D_MODEL, D_FF)), _resident((D_FF, D_MODEL)),
                  _resident((1, D_MODEL)), _resident((D_MODEL, D_IN)),
                  _resident((1, D_GMLP)), _resident((1, D_GMLP)),
                  _resident(w["lbf"].shape), _resident(w["lbb"].shape)],
        out_specs=[tok(TOK_PRE, D_MODEL)] + [tok(TOK_PRE, D_HGRN)] * 7,
        out_shape=[jax.ShapeDtypeStruct((n_tok, D_MODEL), F32)]
                  + [half(dt) for dt in (BF16, BF16, BF16, BF16, F32, F32, BF16)],
        scratch_shapes=[pltpu.VMEM((TOK_PRE, D_MODEL), BF16), pltpu.VMEM((TOK_PRE, D_MODEL), F32)],
        compiler_params=_params(("parallel",)),
        name="pre",
    )(xf, w["n1"], w["wg1"], w["wu1"], w["wd1"], w["nmix"], w["win"], w["lng"], w["lnb"],
      w["lbf"], w["lbb"])
    x1, gu, gv, qs, iv, ff, fb, g = pre_out

    n_blk = seq // TOK_MIX
    seq3 = lambda a: a.reshape(batch, seq, D_HGRN)
    fwd = pl.BlockSpec((MIX_SEQS, TOK_MIX, D_HGRN), lambda b, c: (b, c, 0))
    bwd = pl.BlockSpec((MIX_SEQS, TOK_MIX, D_HGRN), lambda b, c: (b, n_blk - 1 - c, 0))
    seq_shape = lambda dt: jax.ShapeDtypeStruct((batch, seq, D_HGRN), dt)

    def mixer(bounded):
        dfw, dbw = _decay_matrices(bounded)
        return pl.pallas_call(
            functools.partial(_mixer_kernel, bounded=bounded),
            grid=(batch // MIX_SEQS, n_blk),
            in_specs=[fwd, fwd, fwd, fwd, fwd, bwd, bwd, bwd,
                      _resident(w["ws"].shape), _resident(w["bs"].shape), _resident((1, D_GMLP)),
                      _resident(dfw.shape), _resident(dbw.shape)],
            out_specs=[fwd, fwd, bwd],
            out_shape=[seq_shape(BF16), seq_shape(F32), seq_shape(F32)],
            scratch_shapes=[pltpu.VMEM((MIX_SEQS, N_HEADS, HEAD, HEAD), F32),
                            pltpu.VMEM((MIX_SEQS, N_HEADS, HEAD, HEAD), F32)],
            compiler_params=_params(("parallel", "arbitrary")),
            name="mixer_bounded" if bounded else "mixer",
        )(seq3(gu), seq3(gv), seq3(qs), seq3(iv), seq3(ff), seq3(qs), seq3(iv), seq3(fb),
          w["ws"], w["bs"], w["gon"], dfw, dbw)

    yg, of, ob = lax.cond(w["lb_min"] >= np.exp(-MAX_EXP_ARG / HALF),
                          lambda: mixer(False), lambda: mixer(True))

    flat = lambda a: a.reshape(n_tok, D_HGRN)
    out = pl.pallas_call(
        _post_kernel,
        grid=(n_tok // TOK_POST,),
        in_specs=[tok(TOK_POST, D_MODEL)] + [tok(TOK_POST, D_HGRN)] * 4
                 + [_resident((1, D_HGRN)), _resident((D_MODEL, D_MODEL)), _resident((1, D_MODEL)),
                    _resident((D_MODEL, D_FF)), _resident((D_MODEL, D_FF)), _resident((D_FF, D_MODEL)),
                    _resident((1, D_MODEL))],
        out_specs=tok(TOK_POST, D_MODEL),
        out_shape=jax.ShapeDtypeStruct((n_tok, D_MODEL), F32),
        scratch_shapes=[pltpu.VMEM((TOK_POST, D_MODEL), BF16), pltpu.VMEM((TOK_POST, D_MODEL), F32)],
        compiler_params=_params(("parallel",)),
        name="post",
    )(x1, flat(yg), flat(of), flat(ob), g, w["hon"], w["wout"], w["n2"],
      w["wg2"], w["wu2"], w["wd2"], w["nf"])
    return out.reshape(batch, seq, D_MODEL)


def kernel(x_prompt, x_sample, ffn1_norm, ffn1_w_gate, ffn1_w_up, ffn1_w_down, mix_norm, w_in,
           gmlp_ln_g, gmlp_ln_b, gmlp_w_s, gmlp_b_s, gmlp_out_norm,
           hgrn_lb_fwd, hgrn_lb_bwd, hgrn_out_norm, w_out,
           ffn2_norm, ffn2_w_gate, ffn2_w_up, ffn2_w_down, final_norm):
    lb_of = lambda p: jnp.cumsum(jax.nn.softmax(p.astype(F32), axis=0), axis=0)[0]
    w = dict(
        lb_min=jnp.minimum(jnp.min(lb_of(hgrn_lb_fwd)), jnp.min(lb_of(hgrn_lb_bwd))),
        n1=_row(ffn1_norm[0]), wg1=ffn1_w_gate[0].astype(BF16), wu1=ffn1_w_up[0].astype(BF16),
        wd1=ffn1_w_down[0].astype(BF16), nmix=_row(mix_norm[0]), win=w_in[0].astype(BF16),
        lng=_row(gmlp_ln_g[0]), lnb=_row(gmlp_ln_b[0]),
        lbf=hgrn_lb_fwd.astype(F32), lbb=hgrn_lb_bwd.astype(F32),
        ws=gmlp_w_s[0].astype(BF16), bs=gmlp_b_s[0].T.astype(F32), gon=_row(gmlp_out_norm[0]),
        hon=_row(hgrn_out_norm[0]), wout=w_out[0].astype(BF16),
        n2=_row(ffn2_norm[0]), wg2=ffn2_w_gate[0].astype(BF16), wu2=ffn2_w_up[0].astype(BF16),
        wd2=ffn2_w_down[0].astype(BF16), nf=_row(final_norm),
    )
    return _trunk(x_prompt, w), _trunk(x_sample, w)
```

```python
import functools

import jax
import jax.numpy as jnp
import numpy as np
from jax import lax
from jax.experimental import pallas as pl
from jax.experimental.pallas import tpu as pltpu

F32 = jnp.float32
BF16 = jnp.bfloat16

D_MODEL = 1024
D_GMLP = 512
D_HGRN = 512
N_HEADS = 4
HEAD = 128
CHUNK = 128
HALF = CHUNK // 2
D_FF = 2816
D_IN = 2 * D_GMLP + 5 * D_HGRN
EPS = 1e-6

FF_COLS = 256
FF_STEPS = D_FF // FF_COLS
TOK_PRE = 512
TOK_POST = 512
TOK_MIX = 512
MIX_CHUNKS = TOK_MIX // CHUNK
MIX_SEQS = 2
N_LEVELS = 7
MAX_EXP_ARG = 80.0
VMEM_LIMIT = 56 * 1024 * 1024

SQRT_HALF = float(np.sqrt(0.5).astype(np.float32))

NT_DIMS = (((1,), (1,)), ((), ()))
TN_DIMS = (((0,), (0,)), ((), ()))


def _rms(x, gain):
    return x * lax.rsqrt(jnp.mean(x * x, axis=-1, keepdims=True) + EPS) * gain


def _silu(x):
    return x * jax.nn.sigmoid(x)


def _gelu(x):
    return 0.5 * x * (1.0 + lax.erf(x * SQRT_HALF))


def _swiglu(act_ref, hb_ref, wg_ref, wu_ref, wd_ref):
    for c in range(FF_STEPS):
        cols = slice(c * FF_COLS, (c + 1) * FF_COLS)
        g = jnp.dot(hb_ref[...], wg_ref[:, cols], preferred_element_type=F32)
        u = jnp.dot(hb_ref[...], wu_ref[:, cols], preferred_element_type=F32)
        act_ref[:, cols] = (_silu(g) * u).astype(BF16)
    return jnp.dot(act_ref[...], wd_ref[...], preferred_element_type=F32)


def _lower_bound(p):
    e = jnp.exp(p - jnp.max(p, axis=0, keepdims=True))
    return e[0:1, :] / jnp.sum(e, axis=0, keepdims=True)


def _pre_kernel(x_ref, n1_ref, wg_ref, wu_ref, wd_ref, nmix_ref, win_ref, lng_ref, lnb_ref,
                lbf_ref, lbb_ref,
                x1_ref, gu_ref, gv_ref, qs_ref, i_ref, ff_ref, fb_ref, g_ref,
                hb_ref, act_ref):
    x = x_ref[...]
    hb_ref[...] = _rms(x, n1_ref[...]).astype(BF16)
    x1 = x + 0.5 * _swiglu(act_ref, hb_ref, wg_ref, wu_ref, wd_ref)
    x1_ref[...] = x1
    hb_ref[...] = _rms(x1, nmix_ref[...]).astype(BF16)

    def proj(j):
        return jnp.dot(hb_ref[...], win_ref[:, j * D_HGRN:(j + 1) * D_HGRN],
                       preferred_element_type=F32)

    gu_ref[...] = _gelu(proj(0)).astype(BF16)
    v = _gelu(proj(1))
    mu = jnp.mean(v, axis=-1, keepdims=True)
    vc = v - mu
    gv_ref[...] = (vc * lax.rsqrt(jnp.mean(vc * vc, axis=-1, keepdims=True) + EPS)
                   * lng_ref[...] + lnb_ref[...]).astype(BF16)
    qs_ref[...] = _silu(proj(2)).astype(BF16)
    i_ref[...] = proj(3).astype(BF16)
    lbf = _lower_bound(lbf_ref[...])
    ff_ref[...] = lbf + (1.0 - lbf) * jax.nn.sigmoid(proj(4))
    lbb = _lower_bound(lbb_ref[...])
    fb_ref[...] = lbb + (1.0 - lbb) * jax.nn.sigmoid(proj(5))
    g_ref[...] = proj(6).astype(BF16)


def _heads():
    return [slice(h * HEAD, (h + 1) * HEAD) for h in range(N_HEADS)]


def _log_decay(f, dm_ref):
    lf = jnp.log(f)
    hi = lf.astype(BF16)
    lo = (lf - hi.astype(F32)).astype(BF16)
    hilo = jnp.concatenate([hi, lo], axis=0)
    return lf, 1.0 - f, lambda i: jnp.dot(dm_ref[i], hilo, preferred_element_type=F32)


def _single_ref_prep(f, dm_ref, forward):
    lf, k, cum = _log_decay(f, dm_ref)
    sum_a = jnp.sum(lf[:HALF], axis=0, keepdims=True)
    sum_b = jnp.sum(lf[HALF:], axis=0, keepdims=True)
    r_in, r_out = (sum_a, sum_b) if forward else (sum_b, sum_a)
    return k, cum(0), r_in, r_out


def _single_ref_scores(qs, vb, prep, forward):
    k, x, r_in, r_out = prep
    row = lax.broadcasted_iota(jnp.int32, (CHUNK, CHUNK), 0)
    col = lax.broadcasted_iota(jnp.int32, (CHUNK, CHUNK), 1)
    qt = qs * jnp.exp(x)
    kt = k * jnp.exp(-x)
    qh = (qt * jnp.exp(r_in)).astype(BF16)
    kh = (kt * jnp.exp(r_out)).astype(BF16)
    dec = jnp.exp(r_in + r_out)
    qt = qt.astype(BF16)
    kt = kt.astype(BF16)
    visible = (col <= row) if forward else (col >= row)
    scores = [jnp.where(visible,
                        lax.dot_general(qt[:, sl], kt[:, sl], NT_DIMS, preferred_element_type=F32), 0.0)
              for sl in _heads()]
    ds = [lax.dot_general(vb[:, sl], kh[:, sl], TN_DIMS, preferred_element_type=F32) for sl in _heads()]
    return scores, qh, ds, dec, None


def _bounded_scores(qs, v, f, dm_ref, forward):
    lf, k, cum = _log_decay(f, dm_ref)
    row = lax.broadcasted_iota(jnp.int32, (CHUNK, CHUNK), 0)
    col = lax.broadcasted_iota(jnp.int32, (CHUNK, CHUNK), 1)
    tok = lax.broadcasted_iota(jnp.int32, (CHUNK, D_HGRN), 0)
    qh = (qs * jnp.exp(cum(N_LEVELS))).astype(BF16)
    kh = (k * jnp.exp(cum(N_LEVELS + 1))).astype(BF16)
    dec = jnp.exp(jnp.sum(lf, axis=0, keepdims=True))
    scores = [jnp.zeros((CHUNK, CHUNK), F32)] * N_HEADS
    for lvl in range(N_LEVELS):
        m = 1 << lvl
        is_query = ((tok & m) != 0) if forward else ((tok & m) == 0)
        z = (jnp.where(is_query, qs, k) * jnp.exp(-jnp.abs(cum(lvl)))).astype(BF16)
        pair = (((row ^ col) >> lvl) == 1) & ((col < row) if forward else (col > row))
        scores = [sc + jnp.where(pair, lax.dot_general(z[:, sl], z[:, sl], NT_DIMS,
                                                       preferred_element_type=F32), 0.0)
                  for sc, sl in zip(scores, _heads())]
    same_pos = jnp.concatenate(
        [jnp.sum(qs[:, sl] * k[:, sl], axis=-1, keepdims=True) * v[:, sl] for sl in _heads()], axis=-1)
    vb = v.astype(BF16)
    ds = [lax.dot_general(vb[:, sl], kh[:, sl], TN_DIMS, preferred_element_type=F32) for sl in _heads()]
    return scores, qh, ds, dec, same_pos


def _mixer_kernel(gu_ref, gv_ref, qsf_ref, if_ref, ff_ref, qsb_ref, ib_ref, fb_ref,
                  ws_ref, bs_ref, gon_ref, dfw_ref, dbw_ref,
                  yg_ref, of_ref, ob_ref,
                  sf_ref, sb_ref, *, bounded):
    @pl.when(pl.program_id(1) == 0)
    def _():
        sf_ref[...] = jnp.zeros_like(sf_ref)
        sb_ref[...] = jnp.zeros_like(sb_ref)

    chunk_rows = [slice(c * CHUNK, (c + 1) * CHUNK) for c in range(MIX_CHUNKS)]
    streams = ((qsf_ref, if_ref, ff_ref, dfw_ref, True, of_ref, sf_ref, chunk_rows),
               (qsb_ref, ib_ref, fb_ref, dbw_ref, False, ob_ref, sb_ref, chunk_rows[::-1]))
    items = [(s, n) for s in range(MIX_SEQS) for n in range(MIX_CHUNKS)]

    prep = {}
    for s, n in items:
        rows = chunk_rows[n]
        gv = gv_ref[s, rows, :]
        mixed = [jnp.dot(ws_ref[h], gv[:, sl], preferred_element_type=F32) + bs_ref[:, h:h + 1]
                 for h, sl in enumerate(_heads())]
        yg_ref[s, rows, :] = _rms(gu_ref[s, rows, :] * jnp.concatenate(mixed, axis=-1),
                                  gon_ref[...]).astype(BF16)
        if not bounded:
            for d, (_, _, f_ref, dm_ref, fw, _, _, order) in enumerate(streams):
                prep[s, n, d] = _single_ref_prep(f_ref[s, order[n], :], dm_ref, fw)

    mid = {}
    for s, n in items:
        for d, (qs_ref, v_ref, f_ref, dm_ref, fw, _, _, order) in enumerate(streams):
            rows = order[n]
            if bounded:
                mid[s, n, d] = _bounded_scores(qs_ref[s, rows, :], v_ref[s, rows, :], f_ref[s, rows, :],
                                               dm_ref, fw)
            else:
                mid[s, n, d] = _single_ref_scores(qs_ref[s, rows, :], v_ref[s, rows, :],
                                                  prep[s, n, d], fw)

    state = {(s, d): [st_ref[s, h] for h in range(N_HEADS)]
             for s in range(MIX_SEQS) for d, (*_, st_ref, _) in enumerate(streams)}
    for n in range(MIX_CHUNKS):
        for s in range(MIX_SEQS):
            for d, (_, v_ref, _, _, _, o_ref, _, order) in enumerate(streams):
                rows = order[n]
                scores, qh, ds, dec, same_pos = mid[s, n, d]
                vb = v_ref[s, rows, :]
                outs = []
                for h, sl in enumerate(_heads()):
                    st = state[s, d][h]
                    o = jnp.dot(scores[h].astype(BF16), vb[:, sl], preferred_element_type=F32)
                    o += lax.dot_general(qh[:, sl], st.astype(BF16), NT_DIMS,
                                         preferred_element_type=F32)
                    outs.append(o)
                    state[s, d][h] = st * dec[:, sl] + ds[h]
                o = jnp.concatenate(outs, axis=-1)
                o_ref[s, rows, :] = o if same_pos is None else o + same_pos
    for (s, d), st in state.items():
        for h in range(N_HEADS):
            streams[d][6][s, h] = st[h]


def _post_kernel(x1_ref, yg_ref, of_ref, ob_ref, g_ref, hon_ref, wout_ref, n2_ref,
                 wg_ref, wu_ref, wd_ref, nf_ref,
                 out_ref,
                 hb_ref, act_ref):
    o = of_ref[...] + ob_ref[...]
    heads = []
    for h in range(N_HEADS):
        sl = slice(h * HEAD, (h + 1) * HEAD)
        heads.append(_rms(o[:, sl], hon_ref[:, sl]))
    yh = jnp.concatenate(heads, axis=-1) * _silu(g_ref[...].astype(F32))
    x2 = (x1_ref[...]
          + jnp.dot(yg_ref[...], wout_ref[:D_GMLP, :], preferred_element_type=F32)
          + jnp.dot(yh.astype(BF16), wout_ref[D_GMLP:, :], preferred_element_type=F32))
    hb_ref[...] = _rms(x2, n2_ref[...]).astype(BF16)
    x3 = x2 + 0.5 * _swiglu(act_ref, hb_ref, wg_ref, wu_ref, wd_ref)
    out_ref[...] = _rms(x3, nf_ref[...])


def _resident(shape):
    nd = len(shape)
    return pl.BlockSpec(shape, lambda *_: (0,) * nd, pipeline_mode=pl.Buffered(1))


def _decay_matrices(bounded):
    t = np.arange(CHUNK)
    out = []
    for forward in (True, False):
        cums = (t[None, :] <= t[:, None]) if forward else (t[None, :] >= t[:, None])
        cums = cums.astype(np.float32)
        if bounded:
            mats = []
            for lvl in range(N_LEVELS):
                m = 1 << lvl
                ref = (t // (2 * m)) * (2 * m) + (m - 1 if forward else m)
                mats.append(cums - cums[ref])
            mats += [cums, 1.0 - cums]
        else:
            mats = [cums - cums[HALF - 1 if forward else HALF]]
        d = np.stack(mats)
        out.append(jnp.asarray(np.concatenate([d, d], axis=2), dtype=BF16))
    return out


def _row(v):
    return v.reshape(1, -1).astype(F32)


def _params(sem):
    return pltpu.CompilerParams(dimension_semantics=sem, vmem_limit_bytes=VMEM_LIMIT)


def _trunk(x, w):
    batch, seq, _ = x.shape
    n_tok = batch * seq
    assert n_tok % TOK_PRE == 0 and n_tok % TOK_POST == 0 and seq % TOK_MIX == 0
    assert batch % MIX_SEQS == 0
    xf = x.reshape(n_tok, D_MODEL)

    tok = lambda t, d: pl.BlockSpec((t, d), lambda i: (i, 0))
    half = lambda dt: jax.ShapeDtypeStruct((n_tok, D_HGRN), dt)

    pre_out = pl.pallas_call(
        _pre_kernel,
        grid=(n_tok // TOK_PRE,),
        in_specs=[tok(TOK_PRE, D_MODEL), _resident((1, D_MODEL)),
                  _resident((D_MODEL, D_FF)), _resident((D_MODEL, D_FF)), _resident((D_FF, D_MODEL)),
                  _resident((1, D_MODEL)), _resident((D_MODEL, D_IN)),
                  _resident((1, D_GMLP)), _resident((1, D_GMLP)),
                  _resident(w["lbf"].shape), _resident(w["lbb"].shape)],
        out_specs=[tok(TOK_PRE, D_MODEL)] + [tok(TOK_PRE, D_HGRN)] * 7,
        out_shape=[jax.ShapeDtypeStruct((n_tok, D_MODEL), F32)]
                  + [half(dt) for dt in (BF16, BF16, BF16, BF16, F32, F32, BF16)],
        scratch_shapes=[pltpu.VMEM((TOK_PRE, D_MODEL), BF16), pltpu.VMEM((TOK_PRE, D_FF), BF16)],
        compiler_params=_params(("parallel",)),
        name="pre",
    )(xf, w["n1"], w["wg1"], w["wu1"], w["wd1"], w["nmix"], w["win"], w["lng"], w["lnb"],
      w["lbf"], w["lbb"])
    x1, gu, gv, qs, iv, ff, fb, g = pre_out

    n_blk = seq // TOK_MIX
    seq3 = lambda a: a.reshape(batch, seq, D_HGRN)
    fwd = pl.BlockSpec((MIX_SEQS, TOK_MIX, D_HGRN), lambda b, c: (b, c, 0))
    bwd = pl.BlockSpec((MIX_SEQS, TOK_MIX, D_HGRN), lambda b, c: (b, n_blk - 1 - c, 0))
    seq_shape = lambda dt: jax.ShapeDtypeStruct((batch, seq, D_HGRN), dt)

    def mixer(bounded):
        dfw, dbw = _decay_matrices(bounded)
        return pl.pallas_call(
            functools.partial(_mixer_kernel, bounded=bounded),
            grid=(batch // MIX_SEQS, n_blk),
            in_specs=[fwd, fwd, fwd, fwd, fwd, bwd, bwd, bwd,
                      _resident(w["ws"].shape), _resident(w["bs"].shape), _resident((1, D_GMLP)),
                      _resident(dfw.shape), _resident(dbw.shape)],
            out_specs=[fwd, fwd, bwd],
            out_shape=[seq_shape(BF16), seq_shape(F32), seq_shape(F32)],
            scratch_shapes=[pltpu.VMEM((MIX_SEQS, N_HEADS, HEAD, HEAD), F32),
                            pltpu.VMEM((MIX_SEQS, N_HEADS, HEAD, HEAD), F32)],
            compiler_params=_params(("parallel", "arbitrary")),
            name="mixer_bounded" if bounded else "mixer",
        )(seq3(gu), seq3(gv), seq3(qs), seq3(iv), seq3(ff), seq3(qs), seq3(iv), seq3(fb),
          w["ws"], w["bs"], w["gon"], dfw, dbw)

    yg, of, ob = lax.cond(w["lb_min"] >= np.exp(-MAX_EXP_ARG / HALF),
                          lambda: mixer(False), lambda: mixer(True))

    flat = lambda a: a.reshape(n_tok, D_HGRN)
    out = pl.pallas_call(
        _post_kernel,
        grid=(n_tok // TOK_POST,),
        in_specs=[tok(TOK_POST, D_MODEL)] + [tok(TOK_POST, D_HGRN)] * 4
                 + [_resident((1, D_HGRN)), _resident((D_MODEL, D_MODEL)), _resident((1, D_MODEL)),
                    _resident((D_MODEL, D_FF)), _resident((D_MODEL, D_FF)), _resident((D_FF, D_MODEL)),
                    _resident((1, D_MODEL))],
        out_specs=tok(TOK_POST, D_MODEL),
        out_shape=jax.ShapeDtypeStruct((n_tok, D_MODEL), F32),
        scratch_shapes=[pltpu.VMEM((TOK_POST, D_MODEL), BF16), pltpu.VMEM((TOK_POST, D_FF), BF16)],
        compiler_params=_params(("parallel",)),
        name="post",
    )(x1, flat(yg), flat(of), flat(ob), g, w["hon"], w["wout"], w["n2"],
      w["wg2"], w["wu2"], w["wd2"], w["nf"])
    return out.reshape(batch, seq, D_MODEL)


def kernel(x_prompt, x_sample, ffn1_norm, ffn1_w_gate, ffn1_w_up, ffn1_w_down, mix_norm, w_in,
           gmlp_ln_g, gmlp_ln_b, gmlp_w_s, gmlp_b_s, gmlp_out_norm,
           hgrn_lb_fwd, hgrn_lb_bwd, hgrn_out_norm, w_out,
           ffn2_norm, ffn2_w_gate, ffn2_w_up, ffn2_w_down, final_norm):
    lb_of = lambda p: jnp.cumsum(jax.nn.softmax(p.astype(F32), axis=0), axis=0)[0]
    w = dict(
        lb_min=jnp.minimum(jnp.min(lb_of(hgrn_lb_fwd)), jnp.min(lb_of(hgrn_lb_bwd))),
        n1=_row(ffn1_norm[0]), wg1=ffn1_w_gate[0].astype(BF16), wu1=ffn1_w_up[0].astype(BF16),
        wd1=ffn1_w_down[0].astype(BF16), nmix=_row(mix_norm[0]), win=w_in[0].astype(BF16),
        lng=_row(gmlp_ln_g[0]), lnb=_row(gmlp_ln_b[0]),
        lbf=hgrn_lb_fwd.astype(F32), lbb=hgrn_lb_bwd.astype(F32),
        ws=gmlp_w_s[0].astype(BF16), bs=gmlp_b_s[0].T.astype(F32), gon=_row(gmlp_out_norm[0]),
        hon=_row(hgrn_out_norm[0]), wout=w_out[0].astype(BF16),
        n2=_row(ffn2_norm[0]), wg2=ffn2_w_gate[0].astype(BF16), wu2=ffn2_w_up[0].astype(BF16),
        wd2=ffn2_w_down[0].astype(BF16), nf=_row(final_norm),
    )
    return _trunk(x_prompt, w), _trunk(x_sample, w)
```

```python
import jax
import jax.numpy as jnp
import numpy as np
from jax import lax
from jax.experimental import pallas as pl
from jax.experimental.pallas import tpu as pltpu

F32 = jnp.float32
BF16 = jnp.bfloat16

D_MODEL = 1024
D_GMLP = 512
D_HGRN = 512
N_HEADS = 4
HEAD = 128
CHUNK = 128
HALF = CHUNK // 2
D_FF = 2816
D_IN = 2 * D_GMLP + 5 * D_HGRN
EPS = 1e-6

FF_COLS = 256
FF_STEPS = D_FF // FF_COLS
TOK_PRE = 512
TOK_POST = 512
TOK_MIX = 512
MIX_CHUNKS = TOK_MIX // CHUNK
MIX_SEQS = 2
N_LEVELS = 7
MAX_EXP_ARG = 80.0
VMEM_LIMIT = 56 * 1024 * 1024

SQRT_HALF = float(np.sqrt(0.5).astype(np.float32))

NT_DIMS = (((1,), (1,)), ((), ()))
TN_DIMS = (((0,), (0,)), ((), ()))


def _rms(x, gain):
    return x * lax.rsqrt(jnp.mean(x * x, axis=-1, keepdims=True) + EPS) * gain


def _silu(x):
    return x * jax.nn.sigmoid(x)


def _gelu(x):
    return 0.5 * x * (1.0 + lax.erf(x * SQRT_HALF))


def _swiglu(act_ref, hb_ref, wg_ref, wu_ref, wd_ref):
    for c in range(FF_STEPS):
        cols = slice(c * FF_COLS, (c + 1) * FF_COLS)
        g = jnp.dot(hb_ref[...], wg_ref[:, cols], preferred_element_type=F32)
        u = jnp.dot(hb_ref[...], wu_ref[:, cols], preferred_element_type=F32)
        act_ref[:, cols] = (_silu(g) * u).astype(BF16)
    return jnp.dot(act_ref[...], wd_ref[...], preferred_element_type=F32)


def _lower_bound(p):
    e = jnp.exp(p - jnp.max(p, axis=0, keepdims=True))
    return e[0:1, :] / jnp.sum(e, axis=0, keepdims=True)


def _pre_kernel(x_ref, n1_ref, wg_ref, wu_ref, wd_ref, nmix_ref, win_ref, lng_ref, lnb_ref,
                lbf_ref, lbb_ref,
                x1_ref, gu_ref, gv_ref, qs_ref, i_ref, ff_ref, fb_ref, g_ref,
                hb_ref, act_ref):
    x = x_ref[...]
    hb_ref[...] = _rms(x, n1_ref[...]).astype(BF16)
    x1 = x + 0.5 * _swiglu(act_ref, hb_ref, wg_ref, wu_ref, wd_ref)
    x1_ref[...] = x1
    hb_ref[...] = _rms(x1, nmix_ref[...]).astype(BF16)

    def proj(j):
        return jnp.dot(hb_ref[...], win_ref[:, j * D_HGRN:(j + 1) * D_HGRN],
                       preferred_element_type=F32)

    gu_ref[...] = _gelu(proj(0)).astype(BF16)
    v = _gelu(proj(1))
    mu = jnp.mean(v, axis=-1, keepdims=True)
    vc = v - mu
    gv_ref[...] = (vc * lax.rsqrt(jnp.mean(vc * vc, axis=-1, keepdims=True) + EPS)
                   * lng_ref[...] + lnb_ref[...]).astype(BF16)
    qs_ref[...] = _silu(proj(2)).astype(BF16)
    i_ref[...] = proj(3).astype(BF16)
    lbf = _lower_bound(lbf_ref[...])
    ff_ref[...] = lbf + (1.0 - lbf) * jax.nn.sigmoid(proj(4))
    lbb = _lower_bound(lbb_ref[...])
    fb_ref[...] = lbb + (1.0 - lbb) * jax.nn.sigmoid(proj(5))
    g_ref[...] = proj(6).astype(BF16)


def _heads():
    return [slice(h * HEAD, (h + 1) * HEAD) for h in range(N_HEADS)]


def _log_decay(f, dm_ref):
    lf = jnp.log(f)
    hi = lf.astype(BF16)
    lo = (lf - hi.astype(F32)).astype(BF16)
    hilo = jnp.concatenate([hi, lo], axis=0)
    return lf, 1.0 - f, lambda i: jnp.dot(dm_ref[i], hilo, preferred_element_type=F32)


def _single_ref_prep(f, dm_ref, forward):
    lf, k, cum = _log_decay(f, dm_ref)
    sum_a = jnp.sum(lf[:HALF], axis=0, keepdims=True)
    sum_b = jnp.sum(lf[HALF:], axis=0, keepdims=True)
    r_in, r_out = (sum_a, sum_b) if forward else (sum_b, sum_a)
    return k, cum(0), r_in, r_out


def _single_ref_scores(qs, vb, prep, forward):
    k, x, r_in, r_out = prep
    row = lax.broadcasted_iota(jnp.int32, (CHUNK, CHUNK), 0)
    col = lax.broadcasted_iota(jnp.int32, (CHUNK, CHUNK), 1)
    qt = qs * jnp.exp(x)
    kt = k * jnp.exp(-x)
    qh = (qt * jnp.exp(r_in)).astype(BF16)
    kh = (kt * jnp.exp(r_out)).astype(BF16)
    dec = jnp.exp(r_in + r_out)
    qt = qt.astype(BF16)
    kt = kt.astype(BF16)
    visible = (col <= row) if forward else (col >= row)
    scores = [jnp.where(visible,
                        lax.dot_general(qt[:, sl], kt[:, sl], NT_DIMS, preferred_element_type=F32), 0.0)
              for sl in _heads()]
    ds = [lax.dot_general(vb[:, sl], kh[:, sl], TN_DIMS, preferred_element_type=F32) for sl in _heads()]
    return scores, qh, ds, dec, None


def _bounded_scores(qs, v, f, dm_ref, forward):
    lf, k, cum = _log_decay(f, dm_ref)
    row = lax.broadcasted_iota(jnp.int32, (CHUNK, CHUNK), 0)
    col = lax.broadcasted_iota(jnp.int32, (CHUNK, CHUNK), 1)
    tok = lax.broadcasted_iota(jnp.int32, (CHUNK, D_HGRN), 0)
    qh = (qs * jnp.exp(cum(N_LEVELS))).astype(BF16)
    kh = (k * jnp.exp(cum(N_LEVELS + 1))).astype(BF16)
    dec = jnp.exp(jnp.sum(lf, axis=0, keepdims=True))
    scores = [jnp.zeros((CHUNK, CHUNK), F32)] * N_HEADS
    for lvl in range(N_LEVELS):
        m = 1 << lvl
        is_query = ((tok & m) != 0) if forward else ((tok & m) == 0)
        z = (jnp.where(is_query, qs, k) * jnp.exp(-jnp.abs(cum(lvl)))).astype(BF16)
        pair = (((row ^ col) >> lvl) == 1) & ((col < row) if forward else (col > row))
        scores = [sc + jnp.where(pair, lax.dot_general(z[:, sl], z[:, sl], NT_DIMS,
                                                       preferred_element_type=F32), 0.0)
                  for sc, sl in zip(scores, _heads())]
    same_pos = jnp.concatenate(
        [jnp.sum(qs[:, sl] * k[:, sl], axis=-1, keepdims=True) * v[:, sl] for sl in _heads()], axis=-1)
    vb = v.astype(BF16)
    ds = [lax.dot_general(vb[:, sl], kh[:, sl], TN_DIMS, preferred_element_type=F32) for sl in _heads()]
    return scores, qh, ds, dec, same_pos


def _mixer_kernel(single_ref_ok, gu_ref, gv_ref, qsf_ref, if_ref, ff_ref, qsb_ref, ib_ref, fb_ref,
                  ws_ref, bs_ref, gon_ref, dfw_ref, dbw_ref, dfw_bounded_ref, dbw_bounded_ref,
                  yg_ref, of_ref, ob_ref,
                  sf_ref, sb_ref):
    @pl.when(pl.program_id(1) == 0)
    def _():
        sf_ref[...] = jnp.zeros_like(sf_ref)
        sb_ref[...] = jnp.zeros_like(sb_ref)

    refs = (gu_ref, gv_ref, qsf_ref, if_ref, ff_ref, qsb_ref, ib_ref, fb_ref, ws_ref, bs_ref, gon_ref)
    outs = (yg_ref, of_ref, ob_ref, sf_ref, sb_ref)

    @pl.when(single_ref_ok[0] != 0)
    def _():
        _mixer_body(*refs, dfw_ref, dbw_ref, *outs, bounded=False)

    @pl.when(single_ref_ok[0] == 0)
    def _():
        _mixer_body(*refs, dfw_bounded_ref, dbw_bounded_ref, *outs, bounded=True)


def _mixer_body(gu_ref, gv_ref, qsf_ref, if_ref, ff_ref, qsb_ref, ib_ref, fb_ref,
                ws_ref, bs_ref, gon_ref, dfw_ref, dbw_ref,
                yg_ref, of_ref, ob_ref,
                sf_ref, sb_ref, *, bounded):
    chunk_rows = [slice(c * CHUNK, (c + 1) * CHUNK) for c in range(MIX_CHUNKS)]
    streams = ((qsf_ref, if_ref, ff_ref, dfw_ref, True, of_ref, sf_ref, chunk_rows),
               (qsb_ref, ib_ref, fb_ref, dbw_ref, False, ob_ref, sb_ref, chunk_rows[::-1]))
    items = [(s, n) for s in range(MIX_SEQS) for n in range(MIX_CHUNKS)]

    prep = {}
    for s, n in items:
        rows = chunk_rows[n]
        gv = gv_ref[s, rows, :]
        mixed = [jnp.dot(ws_ref[h], gv[:, sl], preferred_element_type=F32) + bs_ref[:, h:h + 1]
                 for h, sl in enumerate(_heads())]
        yg_ref[s, rows, :] = _rms(gu_ref[s, rows, :] * jnp.concatenate(mixed, axis=-1),
                                  gon_ref[...]).astype(BF16)
        if not bounded:
            for d, (_, _, f_ref, dm_ref, fw, _, _, order) in enumerate(streams):
                prep[s, n, d] = _single_ref_prep(f_ref[s, order[n], :], dm_ref, fw)

    mid = {}
    for s, n in items:
        for d, (qs_ref, v_ref, f_ref, dm_ref, fw, _, _, order) in enumerate(streams):
            rows = order[n]
            if bounded:
                mid[s, n, d] = _bounded_scores(qs_ref[s, rows, :], v_ref[s, rows, :], f_ref[s, rows, :],
                                               dm_ref, fw)
            else:
                mid[s, n, d] = _single_ref_scores(qs_ref[s, rows, :], v_ref[s, rows, :],
                                                  prep[s, n, d], fw)

    state = {(s, d): [st_ref[s, h] for h in range(N_HEADS)]
             for s in range(MIX_SEQS) for d, (*_, st_ref, _) in enumerate(streams)}
    for n in range(MIX_CHUNKS):
        for s in range(MIX_SEQS):
            for d, (_, v_ref, _, _, _, o_ref, _, order) in enumerate(streams):
                rows = order[n]
                scores, qh, ds, dec, same_pos = mid[s, n, d]
                vb = v_ref[s, rows, :]
                outs = []
                for h, sl in enumerate(_heads()):
                    st = state[s, d][h]
                    o = jnp.dot(scores[h].astype(BF16), vb[:, sl], preferred_element_type=F32)
                    o += lax.dot_general(qh[:, sl], st.astype(BF16), NT_DIMS,
                                         preferred_element_type=F32)
                    outs.append(o)
                    state[s, d][h] = st * dec[:, sl] + ds[h]
                o = jnp.concatenate(outs, axis=-1)
                o_ref[s, rows, :] = o if same_pos is None else o + same_pos
    for (s, d), st in state.items():
        for h in range(N_HEADS):
            streams[d][6][s, h] = st[h]


def _post_kernel(x1_ref, yg_ref, of_ref, ob_ref, g_ref, hon_ref, wout_ref, n2_ref,
                 wg_ref, wu_ref, wd_ref, nf_ref,
                 out_ref,
                 hb_ref, act_ref):
    o = of_ref[...] + ob_ref[...]
    heads = []
    for h in range(N_HEADS):
        sl = slice(h * HEAD, (h + 1) * HEAD)
        heads.append(_rms(o[:, sl], hon_ref[:, sl]))
    yh = jnp.concatenate(heads, axis=-1) * _silu(g_ref[...].astype(F32))
    x2 = (x1_ref[...]
          + jnp.dot(yg_ref[...], wout_ref[:D_GMLP, :], preferred_element_type=F32)
          + jnp.dot(yh.astype(BF16), wout_ref[D_GMLP:, :], preferred_element_type=F32))
    hb_ref[...] = _rms(x2, n2_ref[...]).astype(BF16)
    x3 = x2 + 0.5 * _swiglu(act_ref, hb_ref, wg_ref, wu_ref, wd_ref)
    out_ref[...] = _rms(x3, nf_ref[...])


def _resident(shape):
    nd = len(shape)
    return pl.BlockSpec(shape, lambda *_: (0,) * nd, pipeline_mode=pl.Buffered(1))


def _decay_matrices(bounded):
    t = np.arange(CHUNK)
    out = []
    for forward in (True, False):
        cums = (t[None, :] <= t[:, None]) if forward else (t[None, :] >= t[:, None])
        cums = cums.astype(np.float32)
        if bounded:
            mats = []
            for lvl in range(N_LEVELS):
                m = 1 << lvl
                ref = (t // (2 * m)) * (2 * m) + (m - 1 if forward else m)
                mats.append(cums - cums[ref])
            mats += [cums, 1.0 - cums]
        else:
            mats = [cums - cums[HALF - 1 if forward else HALF]]
        d = np.stack(mats)
        out.append(jnp.asarray(np.concatenate([d, d], axis=2), dtype=BF16))
    return out


def _row(v):
    return v.reshape(1, -1).astype(F32)


def _params(sem):
    return pltpu.CompilerParams(dimension_semantics=sem, vmem_limit_bytes=VMEM_LIMIT)


def _trunk(x, w):
    batch, seq, _ = x.shape
    n_tok = batch * seq
    assert n_tok % TOK_PRE == 0 and n_tok % TOK_POST == 0 and seq % TOK_MIX == 0
    assert batch % MIX_SEQS == 0
    xf = x.reshape(n_tok, D_MODEL)

    tok = lambda t, d: pl.BlockSpec((t, d), lambda i: (i, 0))
    half = lambda dt: jax.ShapeDtypeStruct((n_tok, D_HGRN), dt)

    pre_out = pl.pallas_call(
        _pre_kernel,
        grid=(n_tok // TOK_PRE,),
        in_specs=[tok(TOK_PRE, D_MODEL), _resident((1, D_MODEL)),
                  _resident((D_MODEL, D_FF)), _resident((D_MODEL, D_FF)), _resident((D_FF, D_MODEL)),
                  _resident((1, D_MODEL)), _resident((D_MODEL, D_IN)),
                  _resident((1, D_GMLP)), _resident((1, D_GMLP)),
                  _resident(w["lbf"].shape), _resident(w["lbb"].shape)],
        out_specs=[tok(TOK_PRE, D_MODEL)] + [tok(TOK_PRE, D_HGRN)] * 7,
        out_shape=[jax.ShapeDtypeStruct((n_tok, D_MODEL), F32)]
                  + [half(dt) for dt in (BF16, BF16, BF16, BF16, F32, F32, BF16)],
        scratch_shapes=[pltpu.VMEM((TOK_PRE, D_MODEL), BF16), pltpu.VMEM((TOK_PRE, D_FF), BF16)],
        compiler_params=_params(("parallel",)),
        name="pre",
    )(xf, w["n1"], w["wg1"], w["wu1"], w["wd1"], w["nmix"], w["win"], w["lng"], w["lnb"],
      w["lbf"], w["lbb"])
    x1, gu, gv, qs, iv, ff, fb, g = pre_out

    n_blk = seq // TOK_MIX
    seq3 = lambda a: a.reshape(batch, seq, D_HGRN)
    fwd = pl.BlockSpec((MIX_SEQS, TOK_MIX, D_HGRN), lambda b, c, _: (b, c, 0))
    bwd = pl.BlockSpec((MIX_SEQS, TOK_MIX, D_HGRN), lambda b, c, _: (b, n_blk - 1 - c, 0))
    seq_shape = lambda dt: jax.ShapeDtypeStruct((batch, seq, D_HGRN), dt)
    dms = _decay_matrices(False) + _decay_matrices(True)
    single_ref_ok = (w["lb_min"] >= np.exp(-MAX_EXP_ARG / HALF)).astype(jnp.int32).reshape(1)
    yg, of, ob = pl.pallas_call(
        _mixer_kernel,
        grid_spec=pltpu.PrefetchScalarGridSpec(
            num_scalar_prefetch=1,
            grid=(batch // MIX_SEQS, n_blk),
            in_specs=[fwd, fwd, fwd, fwd, fwd, bwd, bwd, bwd,
                      _resident(w["ws"].shape), _resident(w["bs"].shape), _resident((1, D_GMLP))]
                     + [_resident(dm.shape) for dm in dms],
            out_specs=[fwd, fwd, bwd],
            scratch_shapes=[pltpu.VMEM((MIX_SEQS, N_HEADS, HEAD, HEAD), F32),
                            pltpu.VMEM((MIX_SEQS, N_HEADS, HEAD, HEAD), F32)]),
        out_shape=[seq_shape(BF16), seq_shape(F32), seq_shape(F32)],
        compiler_params=_params(("parallel", "arbitrary")),
        name="mixer",
    )(single_ref_ok, seq3(gu), seq3(gv), seq3(qs), seq3(iv), seq3(ff), seq3(qs), seq3(iv), seq3(fb),
      w["ws"], w["bs"], w["gon"], *dms)

    flat = lambda a: a.reshape(n_tok, D_HGRN)
    out = pl.pallas_call(
        _post_kernel,
        grid=(n_tok // TOK_POST,),
        in_specs=[tok(TOK_POST, D_MODEL)] + [tok(TOK_POST, D_HGRN)] * 4
                 + [_resident((1, D_HGRN)), _resident((D_MODEL, D_MODEL)), _resident((1, D_MODEL)),
                    _resident((D_MODEL, D_FF)), _resident((D_MODEL, D_FF)), _resident((D_FF, D_MODEL)),
                    _resident((1, D_MODEL))],
        out_specs=tok(TOK_POST, D_MODEL),
        out_shape=jax.ShapeDtypeStruct((n_tok, D_MODEL), F32),
        scratch_shapes=[pltpu.VMEM((TOK_POST, D_MODEL), BF16), pltpu.VMEM((TOK_POST, D_FF), BF16)],
        compiler_params=_params(("parallel",)),
        name="post",
    )(x1, flat(yg), flat(of), flat(ob), g, w["hon"], w["wout"], w["n2"],
      w["wg2"], w["wu2"], w["wd2"], w["nf"])
    return out.reshape(batch, seq, D_MODEL)


def kernel(x_prompt, x_sample, ffn1_norm, ffn1_w_gate, ffn1_w_up, ffn1_w_down, mix_norm, w_in,
           gmlp_ln_g, gmlp_ln_b, gmlp_w_s, gmlp_b_s, gmlp_out_norm,
           hgrn_lb_fwd, hgrn_lb_bwd, hgrn_out_norm, w_out,
           ffn2_norm, ffn2_w_gate, ffn2_w_up, ffn2_w_down, final_norm):
    lb_of = lambda p: jnp.cumsum(jax.nn.softmax(p.astype(F32), axis=0), axis=0)[0]
    w = dict(
        lb_min=jnp.minimum(jnp.min(lb_of(hgrn_lb_fwd)), jnp.min(lb_of(hgrn_lb_bwd))),
        n1=_row(ffn1_norm[0]), wg1=ffn1_w_gate[0].astype(BF16), wu1=ffn1_w_up[0].astype(BF16),
        wd1=ffn1_w_down[0].astype(BF16), nmix=_row(mix_norm[0]), win=w_in[0].astype(BF16),
        lng=_row(gmlp_ln_g[0]), lnb=_row(gmlp_ln_b[0]),
        lbf=hgrn_lb_fwd.astype(F32), lbb=hgrn_lb_bwd.astype(F32),
        ws=gmlp_w_s[0].astype(BF16), bs=gmlp_b_s[0].T.astype(F32), gon=_row(gmlp_out_norm[0]),
        hon=_row(hgrn_out_norm[0]), wout=w_out[0].astype(BF16),
        n2=_row(ffn2_norm[0]), wg2=ffn2_w_gate[0].astype(BF16), wu2=ffn2_w_up[0].astype(BF16),
        wd2=ffn2_w_down[0].astype(BF16), nf=_row(final_norm),
    )
    return _trunk(x_prompt, w), _trunk(x_sample, w)
```

```python
import functools

import jax
import jax.numpy as jnp
import numpy as np
from jax import lax
from jax.experimental import pallas as pl
from jax.experimental.pallas import tpu as pltpu

F32 = jnp.float32
BF16 = jnp.bfloat16

D_MODEL = 1024
D_GMLP = 512
D_HGRN = 512
N_HEADS = 4
HEAD = 128
CHUNK = 128
HALF = CHUNK // 2
D_FF = 2816
D_IN = 2 * D_GMLP + 5 * D_HGRN
EPS = 1e-6

FF_COLS = 256
FF_STEPS = D_FF // FF_COLS
TOK_PRE = 512
TOK_POST = 512
TOK_MIX = 512
MIX_CHUNKS = TOK_MIX // CHUNK
MIX_SEQS = 1
N_LEVELS = 7
MAX_EXP_ARG = 80.0
VMEM_LIMIT = 56 * 1024 * 1024

SQRT_HALF = float(np.sqrt(0.5).astype(np.float32))

NT_DIMS = (((1,), (1,)), ((), ()))
TN_DIMS = (((0,), (0,)), ((), ()))


def _rms(x, gain):
    return x * lax.rsqrt(jnp.mean(x * x, axis=-1, keepdims=True) + EPS) * gain


def _silu(x):
    return x * jax.nn.sigmoid(x)


def _gelu(x):
    return 0.5 * x * (1.0 + lax.erf(x * SQRT_HALF))


def _swiglu(act_ref, hb_ref, wg_ref, wu_ref, wd_ref):
    for c in range(FF_STEPS):
        cols = slice(c * FF_COLS, (c + 1) * FF_COLS)
        g = jnp.dot(hb_ref[...], wg_ref[:, cols], preferred_element_type=F32)
        u = jnp.dot(hb_ref[...], wu_ref[:, cols], preferred_element_type=F32)
        act_ref[:, cols] = (_silu(g) * u).astype(BF16)
    return jnp.dot(act_ref[...], wd_ref[...], preferred_element_type=F32)


def _lower_bound(p):
    e = jnp.exp(p - jnp.max(p, axis=0, keepdims=True))
    return e[0:1, :] / jnp.sum(e, axis=0, keepdims=True)


def _pre_kernel(x_ref, n1_ref, wg_ref, wu_ref, wd_ref, nmix_ref, win_ref, lng_ref, lnb_ref,
                lbf_ref, lbb_ref,
                x1_ref, gu_ref, gv_ref, qs_ref, i_ref, ff_ref, fb_ref, g_ref,
                hb_ref, act_ref):
    x = x_ref[...]
    hb_ref[...] = _rms(x, n1_ref[...]).astype(BF16)
    x1 = x + 0.5 * _swiglu(act_ref, hb_ref, wg_ref, wu_ref, wd_ref)
    x1_ref[...] = x1
    hb_ref[...] = _rms(x1, nmix_ref[...]).astype(BF16)

    def proj(j):
        return jnp.dot(hb_ref[...], win_ref[:, j * D_HGRN:(j + 1) * D_HGRN],
                       preferred_element_type=F32)

    gu_ref[...] = _gelu(proj(0)).astype(BF16)
    v = _gelu(proj(1))
    mu = jnp.mean(v, axis=-1, keepdims=True)
    vc = v - mu
    gv_ref[...] = (vc * lax.rsqrt(jnp.mean(vc * vc, axis=-1, keepdims=True) + EPS)
                   * lng_ref[...] + lnb_ref[...]).astype(BF16)
    qs_ref[...] = _silu(proj(2)).astype(BF16)
    i_ref[...] = proj(3).astype(BF16)
    lbf = _lower_bound(lbf_ref[...])
    ff_ref[...] = lbf + (1.0 - lbf) * jax.nn.sigmoid(proj(4))
    lbb = _lower_bound(lbb_ref[...])
    fb_ref[...] = lbb + (1.0 - lbb) * jax.nn.sigmoid(proj(5))
    g_ref[...] = proj(6).astype(BF16)


def _heads():
    return [slice(h * HEAD, (h + 1) * HEAD) for h in range(N_HEADS)]


def _log_decay(f, dm_ref):
    lf = jnp.log(f)
    hi = lf.astype(BF16)
    lo = (lf - hi.astype(F32)).astype(BF16)
    hilo = jnp.concatenate([hi, lo], axis=0)
    return lf, 1.0 - f, lambda i: jnp.dot(dm_ref[i], hilo, preferred_element_type=F32)


def _single_ref_prep(f, dm_ref, forward):
    lf, k, cum = _log_decay(f, dm_ref)
    sum_a = jnp.sum(lf[:HALF], axis=0, keepdims=True)
    sum_b = jnp.sum(lf[HALF:], axis=0, keepdims=True)
    r_in, r_out = (sum_a, sum_b) if forward else (sum_b, sum_a)
    return k, cum(0), r_in, r_out


def _single_ref_scores(qs, vb, prep, forward):
    k, x, r_in, r_out = prep
    row = lax.broadcasted_iota(jnp.int32, (CHUNK, CHUNK), 0)
    col = lax.broadcasted_iota(jnp.int32, (CHUNK, CHUNK), 1)
    qt = qs * jnp.exp(x)
    kt = k * jnp.exp(-x)
    qh = (qt * jnp.exp(r_in)).astype(BF16)
    kh = (kt * jnp.exp(r_out)).astype(BF16)
    dec = jnp.exp(r_in + r_out)
    qt = qt.astype(BF16)
    kt = kt.astype(BF16)
    visible = (col <= row) if forward else (col >= row)
    scores = [jnp.where(visible,
                        lax.dot_general(qt[:, sl], kt[:, sl], NT_DIMS, preferred_element_type=F32), 0.0)
              for sl in _heads()]
    ds = [lax.dot_general(vb[:, sl], kh[:, sl], TN_DIMS, preferred_element_type=F32) for sl in _heads()]
    return scores, qh, ds, dec, None


def _bounded_scores(qs, v, f, dm_ref, forward):
    lf, k, cum = _log_decay(f, dm_ref)
    row = lax.broadcasted_iota(jnp.int32, (CHUNK, CHUNK), 0)
    col = lax.broadcasted_iota(jnp.int32, (CHUNK, CHUNK), 1)
    tok = lax.broadcasted_iota(jnp.int32, (CHUNK, D_HGRN), 0)
    qh = (qs * jnp.exp(cum(N_LEVELS))).astype(BF16)
    kh = (k * jnp.exp(cum(N_LEVELS + 1))).astype(BF16)
    dec = jnp.exp(jnp.sum(lf, axis=0, keepdims=True))
    scores = [jnp.zeros((CHUNK, CHUNK), F32)] * N_HEADS
    for lvl in range(N_LEVELS):
        m = 1 << lvl
        is_query = ((tok & m) != 0) if forward else ((tok & m) == 0)
        z = (jnp.where(is_query, qs, k) * jnp.exp(-jnp.abs(cum(lvl)))).astype(BF16)
        pair = (((row ^ col) >> lvl) == 1) & ((col < row) if forward else (col > row))
        scores = [sc + jnp.where(pair, lax.dot_general(z[:, sl], z[:, sl], NT_DIMS,
                                                       preferred_element_type=F32), 0.0)
                  for sc, sl in zip(scores, _heads())]
    same_pos = jnp.concatenate(
        [jnp.sum(qs[:, sl] * k[:, sl], axis=-1, keepdims=True) * v[:, sl] for sl in _heads()], axis=-1)
    vb = v.astype(BF16)
    ds = [lax.dot_general(vb[:, sl], kh[:, sl], TN_DIMS, preferred_element_type=F32) for sl in _heads()]
    return scores, qh, ds, dec, same_pos


def _mixer_kernel(gu_ref, gv_ref, qsf_ref, if_ref, ff_ref, qsb_ref, ib_ref, fb_ref,
                  ws_ref, bs_ref, gon_ref, dfw_ref, dbw_ref,
                  yg_ref, of_ref, ob_ref,
                  sf_ref, sb_ref, *, bounded):
    @pl.when(pl.program_id(1) == 0)
    def _():
        sf_ref[...] = jnp.zeros_like(sf_ref)
        sb_ref[...] = jnp.zeros_like(sb_ref)

    chunk_rows = [slice(c * CHUNK, (c + 1) * CHUNK) for c in range(MIX_CHUNKS)]
    streams = ((qsf_ref, if_ref, ff_ref, dfw_ref, True, of_ref, sf_ref, chunk_rows),
               (qsb_ref, ib_ref, fb_ref, dbw_ref, False, ob_ref, sb_ref, chunk_rows[::-1]))
    items = [(s, n) for s in range(MIX_SEQS) for n in range(MIX_CHUNKS)]

    prep = {}
    for s, n in items:
        rows = chunk_rows[n]
        gv = gv_ref[s, rows, :]
        mixed = [jnp.dot(ws_ref[h], gv[:, sl], preferred_element_type=F32) + bs_ref[:, h:h + 1]
                 for h, sl in enumerate(_heads())]
        yg_ref[s, rows, :] = _rms(gu_ref[s, rows, :] * jnp.concatenate(mixed, axis=-1),
                                  gon_ref[...]).astype(BF16)
        if not bounded:
            for d, (_, _, f_ref, dm_ref, fw, _, _, order) in enumerate(streams):
                prep[s, n, d] = _single_ref_prep(f_ref[s, order[n], :], dm_ref, fw)

    mid = {}
    for s, n in items:
        for d, (qs_ref, v_ref, f_ref, dm_ref, fw, _, _, order) in enumerate(streams):
            rows = order[n]
            if bounded:
                mid[s, n, d] = _bounded_scores(qs_ref[s, rows, :], v_ref[s, rows, :], f_ref[s, rows, :],
                                               dm_ref, fw)
            else:
                mid[s, n, d] = _single_ref_scores(qs_ref[s, rows, :], v_ref[s, rows, :],
                                                  prep[s, n, d], fw)

    state = {(s, d): [st_ref[s, h] for h in range(N_HEADS)]
             for s in range(MIX_SEQS) for d, (*_, st_ref, _) in enumerate(streams)}
    for n in range(MIX_CHUNKS):
        for s in range(MIX_SEQS):
            for d, (_, v_ref, _, _, _, o_ref, _, order) in enumerate(streams):
                rows = order[n]
                scores, qh, ds, dec, same_pos = mid[s, n, d]
                vb = v_ref[s, rows, :]
                outs = []
                for h, sl in enumerate(_heads()):
                    st = state[s, d][h]
                    o = jnp.dot(scores[h].astype(BF16), vb[:, sl], preferred_element_type=F32)
                    o += lax.dot_general(qh[:, sl], st.astype(BF16), NT_DIMS,
                                         preferred_element_type=F32)
                    outs.append(o)
                    state[s, d][h] = st * dec[:, sl] + ds[h]
                o = jnp.concatenate(outs, axis=-1)
                o_ref[s, rows, :] = o if same_pos is None else o + same_pos
    for (s, d), st in state.items():
        for h in range(N_HEADS):
            streams[d][6][s, h] = st[h]


def _post_kernel(x1_ref, yg_ref, of_ref, ob_ref, g_ref, hon_ref, wout_ref, n2_ref,
                 wg_ref, wu_ref, wd_ref, nf_ref,
                 out_ref,
                 hb_ref, act_ref):
    o = of_ref[...] + ob_ref[...]
    heads = []
    for h in range(N_HEADS):
        sl = slice(h * HEAD, (h + 1) * HEAD)
        heads.append(_rms(o[:, sl], hon_ref[:, sl]))
    yh = jnp.concatenate(heads, axis=-1) * _silu(g_ref[...].astype(F32))
    x2 = (x1_ref[...]
          + jnp.dot(yg_ref[...], wout_ref[:D_GMLP, :], preferred_element_type=F32)
          + jnp.dot(yh.astype(BF16), wout_ref[D_GMLP:, :], preferred_element_type=F32))
    hb_ref[...] = _rms(x2, n2_ref[...]).astype(BF16)
    x3 = x2 + 0.5 * _swiglu(act_ref, hb_ref, wg_ref, wu_ref, wd_ref)
    out_ref[...] = _rms(x3, nf_ref[...])


def _resident(shape):
    nd = len(shape)
    return pl.BlockSpec(shape, lambda *_: (0,) * nd, pipeline_mode=pl.Buffered(1))


def _decay_matrices(bounded):
    t = np.arange(CHUNK)
    out = []
    for forward in (True, False):
        cums = (t[None, :] <= t[:, None]) if forward else (t[None, :] >= t[:, None])
        cums = cums.astype(np.float32)
        if bounded:
            mats = []
            for lvl in range(N_LEVELS):
                m = 1 << lvl
                ref = (t // (2 * m)) * (2 * m) + (m - 1 if forward else m)
                mats.append(cums - cums[ref])
            mats += [cums, 1.0 - cums]
        else:
            mats = [cums - cums[HALF - 1 if forward else HALF]]
        d = np.stack(mats)
        out.append(jnp.asarray(np.concatenate([d, d], axis=2), dtype=BF16))
    return out


def _row(v):
    return v.reshape(1, -1).astype(F32)


def _params(sem):
    return pltpu.CompilerParams(dimension_semantics=sem, vmem_limit_bytes=VMEM_LIMIT)


def _trunk(x, w):
    batch, seq, _ = x.shape
    n_tok = batch * seq
    assert n_tok % TOK_PRE == 0 and n_tok % TOK_POST == 0 and seq % TOK_MIX == 0
    assert batch % MIX_SEQS == 0
    xf = x.reshape(n_tok, D_MODEL)

    tok = lambda t, d: pl.BlockSpec((t, d), lambda i: (i, 0))
    half = lambda dt: jax.ShapeDtypeStruct((n_tok, D_HGRN), dt)

    pre_out = pl.pallas_call(
        _pre_kernel,
        grid=(n_tok // TOK_PRE,),
        in_specs=[tok(TOK_PRE, D_MODEL), _resident((1, D_MODEL)),
                  _resident((D_MODEL, D_FF)), _resident((D_MODEL, D_FF)), _resident((D_FF, D_MODEL)),
                  _resident((1, D_MODEL)), _resident((D_MODEL, D_IN)),
                  _resident((1, D_GMLP)), _resident((1, D_GMLP)),
                  _resident(w["lbf"].shape), _resident(w["lbb"].shape)],
        out_specs=[tok(TOK_PRE, D_MODEL)] + [tok(TOK_PRE, D_HGRN)] * 7,
        out_shape=[jax.ShapeDtypeStruct((n_tok, D_MODEL), F32)]
                  + [half(dt) for dt in (BF16, BF16, BF16, BF16, F32, F32, BF16)],
        scratch_shapes=[pltpu.VMEM((TOK_PRE, D_MODEL), BF16), pltpu.VMEM((TOK_PRE, D_FF), BF16)],
        compiler_params=_params(("parallel",)),
        name="pre",
    )(xf, w["n1"], w["wg1"], w["wu1"], w["wd1"], w["nmix"], w["win"], w["lng"], w["lnb"],
      w["lbf"], w["lbb"])
    x1, gu, gv, qs, iv, ff, fb, g = pre_out

    n_blk = seq // TOK_MIX
    seq3 = lambda a: a.reshape(batch, seq, D_HGRN)
    fwd = pl.BlockSpec((MIX_SEQS, TOK_MIX, D_HGRN), lambda b, c: (b, c, 0))
    bwd = pl.BlockSpec((MIX_SEQS, TOK_MIX, D_HGRN), lambda b, c: (b, n_blk - 1 - c, 0))
    seq_shape = lambda dt: jax.ShapeDtypeStruct((batch, seq, D_HGRN), dt)

    def mixer(bounded):
        dfw, dbw = _decay_matrices(bounded)
        return pl.pallas_call(
            functools.partial(_mixer_kernel, bounded=bounded),
            grid=(batch // MIX_SEQS, n_blk),
            in_specs=[fwd, fwd, fwd, fwd, fwd, bwd, bwd, bwd,
                      _resident(w["ws"].shape), _resident(w["bs"].shape), _resident((1, D_GMLP)),
                      _resident(dfw.shape), _resident(dbw.shape)],
            out_specs=[fwd, fwd, bwd],
            out_shape=[seq_shape(BF16), seq_shape(F32), seq_shape(F32)],
            scratch_shapes=[pltpu.VMEM((MIX_SEQS, N_HEADS, HEAD, HEAD), F32),
                            pltpu.VMEM((MIX_SEQS, N_HEADS, HEAD, HEAD), F32)],
            compiler_params=_params(("parallel", "arbitrary")),
            name="mixer_bounded" if bounded else "mixer",
        )(seq3(gu), seq3(gv), seq3(qs), seq3(iv), seq3(ff), seq3(qs), seq3(iv), seq3(fb),
          w["ws"], w["bs"], w["gon"], dfw, dbw)

    yg, of, ob = lax.cond(w["lb_min"] >= np.exp(-MAX_EXP_ARG / HALF),
                          lambda: mixer(False), lambda: mixer(True))

    flat = lambda a: a.reshape(n_tok, D_HGRN)
    out = pl.pallas_call(
        _post_kernel,
        grid=(n_tok // TOK_POST,),
        in_specs=[tok(TOK_POST, D_MODEL)] + [tok(TOK_POST, D_HGRN)] * 4
                 + [_resident((1, D_HGRN)), _resident((D_MODEL, D_MODEL)), _resident((1, D_MODEL)),
                    _resident((D_MODEL, D_FF)), _resident((D_MODEL, D_FF)), _resident((D_FF, D_MODEL)),
                    _resident((1, D_MODEL))],
        out_specs=tok(TOK_POST, D_MODEL),
        out_shape=jax.ShapeDtypeStruct((n_tok, D_MODEL), F32),
        scratch_shapes=[pltpu.VMEM((TOK_POST, D_MODEL), BF16), pltpu.VMEM((TOK_POST, D_FF), BF16)],
        compiler_params=_params(("parallel",)),
        name="post",
    )(x1, flat(yg), flat(of), flat(ob), g, w["hon"], w["wout"], w["n2"],
      w["wg2"], w["wu2"], w["wd2"], w["nf"])
    return out.reshape(batch, seq, D_MODEL)


def kernel(x_prompt, x_sample, ffn1_norm, ffn1_w_gate, ffn1_w_up, ffn1_w_down, mix_norm, w_in,
           gmlp_ln_g, gmlp_ln_b, gmlp_w_s, gmlp_b_s, gmlp_out_norm,
           hgrn_lb_fwd, hgrn_lb_bwd, hgrn_out_norm, w_out,
           ffn2_norm, ffn2_w_gate, ffn2_w_up, ffn2_w_down, final_norm):
    lb_of = lambda p: jnp.cumsum(jax.nn.softmax(p.astype(F32), axis=0), axis=0)[0]
    w = dict(
        lb_min=jnp.minimum(jnp.min(lb_of(hgrn_lb_fwd)), jnp.min(lb_of(hgrn_lb_bwd))),
        n1=_row(ffn1_norm[0]), wg1=ffn1_w_gate[0].astype(BF16), wu1=ffn1_w_up[0].astype(BF16),
        wd1=ffn1_w_down[0].astype(BF16), nmix=_row(mix_norm[0]), win=w_in[0].astype(BF16),
        lng=_row(gmlp_ln_g[0]), lnb=_row(gmlp_ln_b[0]),
        lbf=hgrn_lb_fwd.astype(F32), lbb=hgrn_lb_bwd.astype(F32),
        ws=gmlp_w_s[0].astype(BF16), bs=gmlp_b_s[0].T.astype(F32), gon=_row(gmlp_out_norm[0]),
        hon=_row(hgrn_out_norm[0]), wout=w_out[0].astype(BF16),
        n2=_row(ffn2_norm[0]), wg2=ffn2_w_gate[0].astype(BF16), wu2=ffn2_w_up[0].astype(BF16),
        wd2=ffn2_w_down[0].astype(BF16), nf=_row(final_norm),
    )
    return _trunk(x_prompt, w), _trunk(x_sample, w)
```

```python
import functools

import jax
import jax.numpy as jnp
import numpy as np
from jax import lax
from jax.experimental import pallas as pl
from jax.experimental.pallas import tpu as pltpu

F32 = jnp.float32
BF16 = jnp.bfloat16

D_MODEL = 1024
D_GMLP = 512
D_HGRN = 512
N_HEADS = 4
HEAD = 128
CHUNK = 128
HALF = CHUNK // 2
D_FF = 2816
D_IN = 2 * D_GMLP + 5 * D_HGRN
EPS = 1e-6

FF_COLS = 256
FF_STEPS = D_FF // FF_COLS
TOK_PRE = 512
TOK_POST = 512
TOK_MIX = 512
MIX_CHUNKS = TOK_MIX // CHUNK
MIX_SEQS = 1
N_LEVELS = 7
MAX_EXP_ARG = 80.0
VMEM_LIMIT = 56 * 1024 * 1024

SQRT_HALF = float(np.sqrt(0.5).astype(np.float32))

NT_DIMS = (((1,), (1,)), ((), ()))
TN_DIMS = (((0,), (0,)), ((), ()))


def _rms(x, gain):
    return x * lax.rsqrt(jnp.mean(x * x, axis=-1, keepdims=True) + EPS) * gain


def _silu(x):
    return x * jax.nn.sigmoid(x)


def _gelu(x):
    return 0.5 * x * (1.0 + lax.erf(x * SQRT_HALF))


def _swiglu(act_ref, hb_ref, wg_ref, wu_ref, wd_ref):
    for c in range(FF_STEPS):
        cols = slice(c * FF_COLS, (c + 1) * FF_COLS)
        g = jnp.dot(hb_ref[...], wg_ref[:, cols], preferred_element_type=F32)
        u = jnp.dot(hb_ref[...], wu_ref[:, cols], preferred_element_type=F32)
        act_ref[:, cols] = (_silu(g) * u).astype(BF16)
    return jnp.dot(act_ref[...], wd_ref[...], preferred_element_type=F32)


def _lower_bound(p):
    e = jnp.exp(p - jnp.max(p, axis=0, keepdims=True))
    return e[0:1, :] / jnp.sum(e, axis=0, keepdims=True)


def _heads():
    return [slice(h * HEAD, (h + 1) * HEAD) for h in range(N_HEADS)]


def _pre_kernel(x_ref, n1_ref, wg_ref, wu_ref, wd_ref, nmix_ref, win_ref, lng_ref, lnb_ref,
                ws_ref, bs_ref, gon_ref, lbf_ref, lbb_ref,
                x1_ref, yg_ref, qs_ref, i_ref, ff_ref, fb_ref, g_ref,
                hb_ref, act_ref):
    x = x_ref[...]
    hb_ref[...] = _rms(x, n1_ref[...]).astype(BF16)
    x1 = x + 0.5 * _swiglu(act_ref, hb_ref, wg_ref, wu_ref, wd_ref)
    x1_ref[...] = x1
    hb_ref[...] = _rms(x1, nmix_ref[...]).astype(BF16)

    def proj(j):
        return jnp.dot(hb_ref[...], win_ref[:, j * D_HGRN:(j + 1) * D_HGRN],
                       preferred_element_type=F32)

    gu = _gelu(proj(0))
    v = _gelu(proj(1))
    mu = jnp.mean(v, axis=-1, keepdims=True)
    vc = v - mu
    gv = (vc * lax.rsqrt(jnp.mean(vc * vc, axis=-1, keepdims=True) + EPS)
          * lng_ref[...] + lnb_ref[...]).astype(BF16)
    for c in range(TOK_PRE // CHUNK):
        rows = slice(c * CHUNK, (c + 1) * CHUNK)
        mixed = [jnp.dot(ws_ref[h], gv[rows, sl], preferred_element_type=F32) + bs_ref[:, h:h + 1]
                 for h, sl in enumerate(_heads())]
        yg_ref[rows, :] = _rms(gu[rows, :] * jnp.concatenate(mixed, axis=-1), gon_ref[...]).astype(BF16)
    qs_ref[...] = _silu(proj(2)).astype(BF16)
    i_ref[...] = proj(3).astype(BF16)
    lbf = _lower_bound(lbf_ref[...])
    ff_ref[...] = lbf + (1.0 - lbf) * jax.nn.sigmoid(proj(4))
    lbb = _lower_bound(lbb_ref[...])
    fb_ref[...] = lbb + (1.0 - lbb) * jax.nn.sigmoid(proj(5))
    g_ref[...] = proj(6).astype(BF16)


def _log_decay(f, dm_ref):
    lf = jnp.log(f)
    hi = lf.astype(BF16)
    lo = (lf - hi.astype(F32)).astype(BF16)
    hilo = jnp.concatenate([hi, lo], axis=0)
    return lf, 1.0 - f, lambda i: jnp.dot(dm_ref[i], hilo, preferred_element_type=F32)


def _single_ref_prep(f, dm_ref, forward):
    lf, k, cum = _log_decay(f, dm_ref)
    sum_a = jnp.sum(lf[:HALF], axis=0, keepdims=True)
    sum_b = jnp.sum(lf[HALF:], axis=0, keepdims=True)
    r_in, r_out = (sum_a, sum_b) if forward else (sum_b, sum_a)
    return k, cum(0), r_in, r_out


def _single_ref_scores(qs, vb, prep, forward):
    k, x, r_in, r_out = prep
    row = lax.broadcasted_iota(jnp.int32, (CHUNK, CHUNK), 0)
    col = lax.broadcasted_iota(jnp.int32, (CHUNK, CHUNK), 1)
    qt = qs * jnp.exp(x)
    kt = k * jnp.exp(-x)
    qh = (qt * jnp.exp(r_in)).astype(BF16)
    kh = (kt * jnp.exp(r_out)).astype(BF16)
    dec = jnp.exp(r_in + r_out)
    qt = qt.astype(BF16)
    kt = kt.astype(BF16)
    visible = (col <= row) if forward else (col >= row)
    scores = [jnp.where(visible,
                        lax.dot_general(qt[:, sl], kt[:, sl], NT_DIMS, preferred_element_type=F32), 0.0)
              for sl in _heads()]
    ds = [lax.dot_general(vb[:, sl], kh[:, sl], TN_DIMS, preferred_element_type=F32) for sl in _heads()]
    return scores, qh, ds, dec, None


def _bounded_scores(qs, v, f, dm_ref, forward):
    lf, k, cum = _log_decay(f, dm_ref)
    row = lax.broadcasted_iota(jnp.int32, (CHUNK, CHUNK), 0)
    col = lax.broadcasted_iota(jnp.int32, (CHUNK, CHUNK), 1)
    tok = lax.broadcasted_iota(jnp.int32, (CHUNK, D_HGRN), 0)
    qh = (qs * jnp.exp(cum(N_LEVELS))).astype(BF16)
    kh = (k * jnp.exp(cum(N_LEVELS + 1))).astype(BF16)
    dec = jnp.exp(jnp.sum(lf, axis=0, keepdims=True))
    scores = [jnp.zeros((CHUNK, CHUNK), F32)] * N_HEADS
    for lvl in range(N_LEVELS):
        m = 1 << lvl
        is_query = ((tok & m) != 0) if forward else ((tok & m) == 0)
        z = (jnp.where(is_query, qs, k) * jnp.exp(-jnp.abs(cum(lvl)))).astype(BF16)
        pair = (((row ^ col) >> lvl) == 1) & ((col < row) if forward else (col > row))
        scores = [sc + jnp.where(pair, lax.dot_general(z[:, sl], z[:, sl], NT_DIMS,
                                                       preferred_element_type=F32), 0.0)
                  for sc, sl in zip(scores, _heads())]
    same_pos = jnp.concatenate(
        [jnp.sum(qs[:, sl] * k[:, sl], axis=-1, keepdims=True) * v[:, sl] for sl in _heads()], axis=-1)
    vb = v.astype(BF16)
    ds = [lax.dot_general(vb[:, sl], kh[:, sl], TN_DIMS, preferred_element_type=F32) for sl in _heads()]
    return scores, qh, ds, dec, same_pos


def _mixer_kernel(qsf_ref, if_ref, ff_ref, qsb_ref, ib_ref, fb_ref, dfw_ref, dbw_ref,
                  of_ref, ob_ref,
                  sf_ref, sb_ref, *, bounded):
    @pl.when(pl.program_id(1) == 0)
    def _():
        sf_ref[...] = jnp.zeros_like(sf_ref)
        sb_ref[...] = jnp.zeros_like(sb_ref)

    chunk_rows = [slice(c * CHUNK, (c + 1) * CHUNK) for c in range(MIX_CHUNKS)]
    streams = ((qsf_ref, if_ref, ff_ref, dfw_ref, True, of_ref, sf_ref, chunk_rows),
               (qsb_ref, ib_ref, fb_ref, dbw_ref, False, ob_ref, sb_ref, chunk_rows[::-1]))
    items = [(s, n) for s in range(MIX_SEQS) for n in range(MIX_CHUNKS)]

    prep = {}
    for s, n in items:
        if not bounded:
            for d, (_, _, f_ref, dm_ref, fw, _, _, order) in enumerate(streams):
                prep[s, n, d] = _single_ref_prep(f_ref[s, order[n], :], dm_ref, fw)

    mid = {}
    for s, n in items:
        for d, (qs_ref, v_ref, f_ref, dm_ref, fw, _, _, order) in enumerate(streams):
            rows = order[n]
            if bounded:
                mid[s, n, d] = _bounded_scores(qs_ref[s, rows, :], v_ref[s, rows, :], f_ref[s, rows, :],
                                               dm_ref, fw)
            else:
                mid[s, n, d] = _single_ref_scores(qs_ref[s, rows, :], v_ref[s, rows, :],
                                                  prep[s, n, d], fw)

    state = {(s, d): [st_ref[s, h] for h in range(N_HEADS)]
             for s in range(MIX_SEQS) for d, (*_, st_ref, _) in enumerate(streams)}
    for n in range(MIX_CHUNKS):
        for s in range(MIX_SEQS):
            for d, (_, v_ref, _, _, _, o_ref, _, order) in enumerate(streams):
                rows = order[n]
                scores, qh, ds, dec, same_pos = mid[s, n, d]
                vb = v_ref[s, rows, :]
                outs = []
                for h, sl in enumerate(_heads()):
                    st = state[s, d][h]
                    o = jnp.dot(scores[h].astype(BF16), vb[:, sl], preferred_element_type=F32)
                    o += lax.dot_general(qh[:, sl], st.astype(BF16), NT_DIMS,
                                         preferred_element_type=F32)
                    outs.append(o)
                    state[s, d][h] = st * dec[:, sl] + ds[h]
                o = jnp.concatenate(outs, axis=-1)
                o_ref[s, rows, :] = o if same_pos is None else o + same_pos
    for (s, d), st in state.items():
        for h in range(N_HEADS):
            streams[d][6][s, h] = st[h]


def _post_kernel(x1_ref, yg_ref, of_ref, ob_ref, g_ref, hon_ref, wout_ref, n2_ref,
                 wg_ref, wu_ref, wd_ref, nf_ref,
                 out_ref,
                 hb_ref, act_ref):
    o = of_ref[...] + ob_ref[...]
    heads = []
    for h in range(N_HEADS):
        sl = slice(h * HEAD, (h + 1) * HEAD)
        heads.append(_rms(o[:, sl], hon_ref[:, sl]))
    yh = jnp.concatenate(heads, axis=-1) * _silu(g_ref[...].astype(F32))
    x2 = (x1_ref[...]
          + jnp.dot(yg_ref[...], wout_ref[:D_GMLP, :], preferred_element_type=F32)
          + jnp.dot(yh.astype(BF16), wout_ref[D_GMLP:, :], preferred_element_type=F32))
    hb_ref[...] = _rms(x2, n2_ref[...]).astype(BF16)
    x3 = x2 + 0.5 * _swiglu(act_ref, hb_ref, wg_ref, wu_ref, wd_ref)
    out_ref[...] = _rms(x3, nf_ref[...])


def _resident(shape):
    nd = len(shape)
    return pl.BlockSpec(shape, lambda *_: (0,) * nd, pipeline_mode=pl.Buffered(1))


def _decay_matrices(bounded):
    t = np.arange(CHUNK)
    out = []
    for forward in (True, False):
        cums = (t[None, :] <= t[:, None]) if forward else (t[None, :] >= t[:, None])
        cums = cums.astype(np.float32)
        if bounded:
            mats = []
            for lvl in range(N_LEVELS):
                m = 1 << lvl
                ref = (t // (2 * m)) * (2 * m) + (m - 1 if forward else m)
                mats.append(cums - cums[ref])
            mats += [cums, 1.0 - cums]
        else:
            mats = [cums - cums[HALF - 1 if forward else HALF]]
        d = np.stack(mats)
        out.append(jnp.asarray(np.concatenate([d, d], axis=2), dtype=BF16))
    return out


def _row(v):
    return v.reshape(1, -1).astype(F32)


def _params(sem):
    return pltpu.CompilerParams(dimension_semantics=sem, vmem_limit_bytes=VMEM_LIMIT)


def _trunk(x, w):
    batch, seq, _ = x.shape
    n_tok = batch * seq
    assert n_tok % TOK_PRE == 0 and n_tok % TOK_POST == 0 and seq % TOK_MIX == 0
    assert batch % MIX_SEQS == 0
    xf = x.reshape(n_tok, D_MODEL)

    tok = lambda t, d: pl.BlockSpec((t, d), lambda i: (i, 0))
    half = lambda dt: jax.ShapeDtypeStruct((n_tok, D_HGRN), dt)

    pre_out = pl.pallas_call(
        _pre_kernel,
        grid=(n_tok // TOK_PRE,),
        in_specs=[tok(TOK_PRE, D_MODEL), _resident((1, D_MODEL)),
                  _resident((D_MODEL, D_FF)), _resident((D_MODEL, D_FF)), _resident((D_FF, D_MODEL)),
                  _resident((1, D_MODEL)), _resident((D_MODEL, D_IN)),
                  _resident((1, D_GMLP)), _resident((1, D_GMLP)),
                  _resident(w["ws"].shape), _resident(w["bs"].shape), _resident((1, D_GMLP)),
                  _resident(w["lbf"].shape), _resident(w["lbb"].shape)],
        out_specs=[tok(TOK_PRE, D_MODEL)] + [tok(TOK_PRE, D_HGRN)] * 6,
        out_shape=[jax.ShapeDtypeStruct((n_tok, D_MODEL), F32)]
                  + [half(dt) for dt in (BF16, BF16, BF16, F32, F32, BF16)],
        scratch_shapes=[pltpu.VMEM((TOK_PRE, D_MODEL), BF16), pltpu.VMEM((TOK_PRE, D_FF), BF16)],
        compiler_params=_params(("parallel",)),
        name="pre",
    )(xf, w["n1"], w["wg1"], w["wu1"], w["wd1"], w["nmix"], w["win"], w["lng"], w["lnb"],
      w["ws"], w["bs"], w["gon"], w["lbf"], w["lbb"])
    x1, yg, qs, iv, ff, fb, g = pre_out

    n_blk = seq // TOK_MIX
    seq3 = lambda a: a.reshape(batch, seq, D_HGRN)
    fwd = pl.BlockSpec((MIX_SEQS, TOK_MIX, D_HGRN), lambda b, c: (b, c, 0))
    bwd = pl.BlockSpec((MIX_SEQS, TOK_MIX, D_HGRN), lambda b, c: (b, n_blk - 1 - c, 0))
    seq_shape = lambda dt: jax.ShapeDtypeStruct((batch, seq, D_HGRN), dt)

    def mixer(bounded):
        dfw, dbw = _decay_matrices(bounded)
        return pl.pallas_call(
            functools.partial(_mixer_kernel, bounded=bounded),
            grid=(batch // MIX_SEQS, n_blk),
            in_specs=[fwd, fwd, fwd, bwd, bwd, bwd, _resident(dfw.shape), _resident(dbw.shape)],
            out_specs=[fwd, bwd],
            out_shape=[seq_shape(F32), seq_shape(F32)],
            scratch_shapes=[pltpu.VMEM((MIX_SEQS, N_HEADS, HEAD, HEAD), F32),
                            pltpu.VMEM((MIX_SEQS, N_HEADS, HEAD, HEAD), F32)],
            compiler_params=_params(("parallel", "arbitrary")),
            name="mixer_bounded" if bounded else "mixer",
        )(seq3(qs), seq3(iv), seq3(ff), seq3(qs), seq3(iv), seq3(fb), dfw, dbw)

    of, ob = lax.cond(w["lb_min"] >= np.exp(-MAX_EXP_ARG / HALF),
                      lambda: mixer(False), lambda: mixer(True))

    flat = lambda a: a.reshape(n_tok, D_HGRN)
    out = pl.pallas_call(
        _post_kernel,
        grid=(n_tok // TOK_POST,),
        in_specs=[tok(TOK_POST, D_MODEL)] + [tok(TOK_POST, D_HGRN)] * 4
                 + [_resident((1, D_HGRN)), _resident((D_MODEL, D_MODEL)), _resident((1, D_MODEL)),
                    _resident((D_MODEL, D_FF)), _resident((D_MODEL, D_FF)), _resident((D_FF, D_MODEL)),
                    _resident((1, D_MODEL))],
        out_specs=tok(TOK_POST, D_MODEL),
        out_shape=jax.ShapeDtypeStruct((n_tok, D_MODEL), F32),
        scratch_shapes=[pltpu.VMEM((TOK_POST, D_MODEL), BF16), pltpu.VMEM((TOK_POST, D_FF), BF16)],
        compiler_params=_params(("parallel",)),
        name="post",
    )(x1, yg, flat(of), flat(ob), g, w["hon"], w["wout"], w["n2"],
      w["wg2"], w["wu2"], w["wd2"], w["nf"])
    return out.reshape(batch, seq, D_MODEL)


def kernel(x_prompt, x_sample, ffn1_norm, ffn1_w_gate, ffn1_w_up, ffn1_w_down, mix_norm, w_in,
           gmlp_ln_g, gmlp_ln_b, gmlp_w_s, gmlp_b_s, gmlp_out_norm,
           hgrn_lb_fwd, hgrn_lb_bwd, hgrn_out_norm, w_out,
           ffn2_norm, ffn2_w_gate, ffn2_w_up, ffn2_w_down, final_norm):
    lb_of = lambda p: jnp.cumsum(jax.nn.softmax(p.astype(F32), axis=0), axis=0)[0]
    w = dict(
        lb_min=jnp.minimum(jnp.min(lb_of(hgrn_lb_fwd)), jnp.min(lb_of(hgrn_lb_bwd))),
        n1=_row(ffn1_norm[0]), wg1=ffn1_w_gate[0].astype(BF16), wu1=ffn1_w_up[0].astype(BF16),
        wd1=ffn1_w_down[0].astype(BF16), nmix=_row(mix_norm[0]), win=w_in[0].astype(BF16),
        lng=_row(gmlp_ln_g[0]), lnb=_row(gmlp_ln_b[0]),
        lbf=hgrn_lb_fwd.astype(F32), lbb=hgrn_lb_bwd.astype(F32),
        ws=gmlp_w_s[0].astype(BF16), bs=gmlp_b_s[0].T.astype(F32), gon=_row(gmlp_out_norm[0]),
        hon=_row(hgrn_out_norm[0]), wout=w_out[0].astype(BF16),
        n2=_row(ffn2_norm[0]), wg2=ffn2_w_gate[0].astype(BF16), wu2=ffn2_w_up[0].astype(BF16),
        wd2=ffn2_w_down[0].astype(BF16), nf=_row(final_norm),
    )
    return _trunk(x_prompt, w), _trunk(x_sample, w)
```

```python
import functools

import jax
import jax.numpy as jnp
import numpy as np
from jax import lax
from jax.experimental import pallas as pl
from jax.experimental.pallas import tpu as pltpu

F32 = jnp.float32
BF16 = jnp.bfloat16

D_MODEL = 1024
D_GMLP = 512
D_HGRN = 512
N_HEADS = 4
HEAD = 128
CHUNK = 128
HALF = CHUNK // 2
D_FF = 2816
D_IN = 2 * D_GMLP + 5 * D_HGRN
EPS = 1e-6

FF_COLS = 256
FF_STEPS = D_FF // FF_COLS
TOK_PRE = 512
TOK_POST = 512
TOK_MIX = 512
MIX_CHUNKS = TOK_MIX // CHUNK
MIX_SEQS = 1
N_LEVELS = 7
MAX_EXP_ARG = 80.0
VMEM_LIMIT = 56 * 1024 * 1024

SQRT_HALF = float(np.sqrt(0.5).astype(np.float32))

NT_DIMS = (((1,), (1,)), ((), ()))
TN_DIMS = (((0,), (0,)), ((), ()))


def _rms(x, gain):
    return x * lax.rsqrt(jnp.mean(x * x, axis=-1, keepdims=True) + EPS) * gain


def _silu(x):
    return x * jax.nn.sigmoid(x)


def _gelu(x):
    return 0.5 * x * (1.0 + lax.erf(x * SQRT_HALF))


def _swiglu(act_ref, hb_ref, wg_ref, wu_ref, wd_ref):
    for c in range(FF_STEPS):
        cols = slice(c * FF_COLS, (c + 1) * FF_COLS)
        g = jnp.dot(hb_ref[...], wg_ref[:, cols], preferred_element_type=F32)
        u = jnp.dot(hb_ref[...], wu_ref[:, cols], preferred_element_type=F32)
        act_ref[:, cols] = (_silu(g) * u).astype(BF16)
    return jnp.dot(act_ref[...], wd_ref[...], preferred_element_type=F32)


def _lower_bound(p):
    e = jnp.exp(p - jnp.max(p, axis=0, keepdims=True))
    return e[0:1, :] / jnp.sum(e, axis=0, keepdims=True)


def _pre_kernel(x_ref, n1_ref, wg_ref, wu_ref, wd_ref, nmix_ref, win_ref, lng_ref, lnb_ref,
                lbf_ref, lbb_ref,
                x1_ref, gg_ref, qi_ref, g_ref, f_ref,
                hb_ref, act_ref):
    x = x_ref[...]
    hb_ref[...] = _rms(x, n1_ref[...]).astype(BF16)
    x1 = x + 0.5 * _swiglu(act_ref, hb_ref, wg_ref, wu_ref, wd_ref)
    x1_ref[...] = x1
    hb_ref[...] = _rms(x1, nmix_ref[...]).astype(BF16)

    def proj(j):
        return jnp.dot(hb_ref[...], win_ref[:, j * D_HGRN:(j + 1) * D_HGRN],
                       preferred_element_type=F32)

    lo, hi = slice(0, D_HGRN), slice(D_HGRN, 2 * D_HGRN)
    gg_ref[:, lo] = _gelu(proj(0)).astype(BF16)
    v = _gelu(proj(1))
    mu = jnp.mean(v, axis=-1, keepdims=True)
    vc = v - mu
    gg_ref[:, hi] = (vc * lax.rsqrt(jnp.mean(vc * vc, axis=-1, keepdims=True) + EPS)
                     * lng_ref[...] + lnb_ref[...]).astype(BF16)
    qi_ref[:, lo] = _silu(proj(2)).astype(BF16)
    qi_ref[:, hi] = proj(3).astype(BF16)
    lbf = _lower_bound(lbf_ref[...])
    f_ref[:, lo] = lbf + (1.0 - lbf) * jax.nn.sigmoid(proj(4))
    lbb = _lower_bound(lbb_ref[...])
    f_ref[:, hi] = lbb + (1.0 - lbb) * jax.nn.sigmoid(proj(5))
    g_ref[...] = proj(6).astype(BF16)


def _heads():
    return [slice(h * HEAD, (h + 1) * HEAD) for h in range(N_HEADS)]


def _log_decay(f, dm_ref):
    lf = jnp.log(f)
    hi = lf.astype(BF16)
    lo = (lf - hi.astype(F32)).astype(BF16)
    hilo = jnp.concatenate([hi, lo], axis=0)
    return lf, 1.0 - f, lambda i: jnp.dot(dm_ref[i], hilo, preferred_element_type=F32)


def _single_ref_prep(f, dm_ref, forward):
    lf, k, cum = _log_decay(f, dm_ref)
    sum_a = jnp.sum(lf[:HALF], axis=0, keepdims=True)
    sum_b = jnp.sum(lf[HALF:], axis=0, keepdims=True)
    r_in, r_out = (sum_a, sum_b) if forward else (sum_b, sum_a)
    return k, cum(0), r_in, r_out


def _single_ref_scores(qs, vb, prep, forward):
    k, x, r_in, r_out = prep
    row = lax.broadcasted_iota(jnp.int32, (CHUNK, CHUNK), 0)
    col = lax.broadcasted_iota(jnp.int32, (CHUNK, CHUNK), 1)
    qt = qs * jnp.exp(x)
    kt = k * jnp.exp(-x)
    qh = (qt * jnp.exp(r_in)).astype(BF16)
    kh = (kt * jnp.exp(r_out)).astype(BF16)
    dec = jnp.exp(r_in + r_out)
    qt = qt.astype(BF16)
    kt = kt.astype(BF16)
    visible = (col <= row) if forward else (col >= row)
    scores = [jnp.where(visible,
                        lax.dot_general(qt[:, sl], kt[:, sl], NT_DIMS, preferred_element_type=F32), 0.0)
              for sl in _heads()]
    ds = [lax.dot_general(vb[:, sl], kh[:, sl], TN_DIMS, preferred_element_type=F32) for sl in _heads()]
    return scores, qh, ds, dec, None


def _bounded_scores(qs, v, f, dm_ref, forward):
    lf, k, cum = _log_decay(f, dm_ref)
    row = lax.broadcasted_iota(jnp.int32, (CHUNK, CHUNK), 0)
    col = lax.broadcasted_iota(jnp.int32, (CHUNK, CHUNK), 1)
    tok = lax.broadcasted_iota(jnp.int32, (CHUNK, D_HGRN), 0)
    qh = (qs * jnp.exp(cum(N_LEVELS))).astype(BF16)
    kh = (k * jnp.exp(cum(N_LEVELS + 1))).astype(BF16)
    dec = jnp.exp(jnp.sum(lf, axis=0, keepdims=True))
    scores = [jnp.zeros((CHUNK, CHUNK), F32)] * N_HEADS
    for lvl in range(N_LEVELS):
        m = 1 << lvl
        is_query = ((tok & m) != 0) if forward else ((tok & m) == 0)
        z = (jnp.where(is_query, qs, k) * jnp.exp(-jnp.abs(cum(lvl)))).astype(BF16)
        pair = (((row ^ col) >> lvl) == 1) & ((col < row) if forward else (col > row))
        scores = [sc + jnp.where(pair, lax.dot_general(z[:, sl], z[:, sl], NT_DIMS,
                                                       preferred_element_type=F32), 0.0)
                  for sc, sl in zip(scores, _heads())]
    same_pos = jnp.concatenate(
        [jnp.sum(qs[:, sl] * k[:, sl], axis=-1, keepdims=True) * v[:, sl] for sl in _heads()], axis=-1)
    vb = v.astype(BF16)
    ds = [lax.dot_general(vb[:, sl], kh[:, sl], TN_DIMS, preferred_element_type=F32) for sl in _heads()]
    return scores, qh, ds, dec, same_pos


def _mixer_kernel(gg_ref, qif_ref, ff_ref, qib_ref, fb_ref,
                  ws_ref, bs_ref, gon_ref, dfw_ref, dbw_ref,
                  yg_ref, of_ref, ob_ref,
                  sf_ref, sb_ref, *, bounded):
    @pl.when(pl.program_id(1) == 0)
    def _():
        sf_ref[...] = jnp.zeros_like(sf_ref)
        sb_ref[...] = jnp.zeros_like(sb_ref)

    chunk_rows = [slice(c * CHUNK, (c + 1) * CHUNK) for c in range(MIX_CHUNKS)]
    lo, hi = slice(0, D_HGRN), slice(D_HGRN, 2 * D_HGRN)
    streams = ((qif_ref, ff_ref, dfw_ref, True, of_ref, sf_ref, chunk_rows),
               (qib_ref, fb_ref, dbw_ref, False, ob_ref, sb_ref, chunk_rows[::-1]))
    items = [(s, n) for s in range(MIX_SEQS) for n in range(MIX_CHUNKS)]

    prep = {}
    for s, n in items:
        rows = chunk_rows[n]
        gv = gg_ref[s, rows, hi]
        mixed = [jnp.dot(ws_ref[h], gv[:, sl], preferred_element_type=F32) + bs_ref[:, h:h + 1]
                 for h, sl in enumerate(_heads())]
        yg_ref[s, rows, :] = _rms(gg_ref[s, rows, lo] * jnp.concatenate(mixed, axis=-1),
                                  gon_ref[...]).astype(BF16)
        if not bounded:
            for d, (_, f_ref, dm_ref, fw, _, _, order) in enumerate(streams):
                prep[s, n, d] = _single_ref_prep(f_ref[s, order[n], :], dm_ref, fw)

    mid = {}
    for s, n in items:
        for d, (qi_ref, f_ref, dm_ref, fw, _, _, order) in enumerate(streams):
            rows = order[n]
            if bounded:
                mid[s, n, d] = _bounded_scores(qi_ref[s, rows, lo], qi_ref[s, rows, hi], f_ref[s, rows, :],
                                               dm_ref, fw)
            else:
                mid[s, n, d] = _single_ref_scores(qi_ref[s, rows, lo], qi_ref[s, rows, hi],
                                                  prep[s, n, d], fw)

    state = {(s, d): [st_ref[s, h] for h in range(N_HEADS)]
             for s in range(MIX_SEQS) for d, (*_, st_ref, _) in enumerate(streams)}
    for n in range(MIX_CHUNKS):
        for s in range(MIX_SEQS):
            for d, (qi_ref, _, _, _, o_ref, _, order) in enumerate(streams):
                rows = order[n]
                scores, qh, ds, dec, same_pos = mid[s, n, d]
                vb = qi_ref[s, rows, hi]
                outs = []
                for h, sl in enumerate(_heads()):
                    st = state[s, d][h]
                    o = jnp.dot(scores[h].astype(BF16), vb[:, sl], preferred_element_type=F32)
                    o += lax.dot_general(qh[:, sl], st.astype(BF16), NT_DIMS,
                                         preferred_element_type=F32)
                    outs.append(o)
                    state[s, d][h] = st * dec[:, sl] + ds[h]
                o = jnp.concatenate(outs, axis=-1)
                o_ref[s, rows, :] = o if same_pos is None else o + same_pos
    for (s, d), st in state.items():
        for h in range(N_HEADS):
            streams[d][5][s, h] = st[h]


def _post_kernel(x1_ref, yg_ref, of_ref, ob_ref, g_ref, hon_ref, wout_ref, n2_ref,
                 wg_ref, wu_ref, wd_ref, nf_ref,
                 out_ref,
                 hb_ref, act_ref):
    o = of_ref[...] + ob_ref[...]
    heads = []
    for h in range(N_HEADS):
        sl = slice(h * HEAD, (h + 1) * HEAD)
        heads.append(_rms(o[:, sl], hon_ref[:, sl]))
    yh = jnp.concatenate(heads, axis=-1) * _silu(g_ref[...].astype(F32))
    x2 = (x1_ref[...]
          + jnp.dot(yg_ref[...], wout_ref[:D_GMLP, :], preferred_element_type=F32)
          + jnp.dot(yh.astype(BF16), wout_ref[D_GMLP:, :], preferred_element_type=F32))
    hb_ref[...] = _rms(x2, n2_ref[...]).astype(BF16)
    x3 = x2 + 0.5 * _swiglu(act_ref, hb_ref, wg_ref, wu_ref, wd_ref)
    out_ref[...] = _rms(x3, nf_ref[...])


def _resident(shape):
    nd = len(shape)
    return pl.BlockSpec(shape, lambda *_: (0,) * nd, pipeline_mode=pl.Buffered(1))


def _decay_matrices(bounded):
    t = np.arange(CHUNK)
    out = []
    for forward in (True, False):
        cums = (t[None, :] <= t[:, None]) if forward else (t[None, :] >= t[:, None])
        cums = cums.astype(np.float32)
        if bounded:
            mats = []
            for lvl in range(N_LEVELS):
                m = 1 << lvl
                ref = (t // (2 * m)) * (2 * m) + (m - 1 if forward else m)
                mats.append(cums - cums[ref])
            mats += [cums, 1.0 - cums]
        else:
            mats = [cums - cums[HALF - 1 if forward else HALF]]
        d = np.stack(mats)
        out.append(jnp.asarray(np.concatenate([d, d], axis=2), dtype=BF16))
    return out


def _row(v):
    return v.reshape(1, -1).astype(F32)


def _params(sem):
    return pltpu.CompilerParams(dimension_semantics=sem, vmem_limit_bytes=VMEM_LIMIT)


def _trunk(x, w):
    batch, seq, _ = x.shape
    n_tok = batch * seq
    assert n_tok % TOK_PRE == 0 and n_tok % TOK_POST == 0 and seq % TOK_MIX == 0
    assert batch % MIX_SEQS == 0
    xf = x.reshape(n_tok, D_MODEL)

    tok = lambda t, d: pl.BlockSpec((t, d), lambda i: (i, 0))
    pair = lambda dt: jax.ShapeDtypeStruct((n_tok, 2 * D_HGRN), dt)

    pre_out = pl.pallas_call(
        _pre_kernel,
        grid=(n_tok // TOK_PRE,),
        in_specs=[tok(TOK_PRE, D_MODEL), _resident((1, D_MODEL)),
                  _resident((D_MODEL, D_FF)), _resident((D_MODEL, D_FF)), _resident((D_FF, D_MODEL)),
                  _resident((1, D_MODEL)), _resident((D_MODEL, D_IN)),
                  _resident((1, D_GMLP)), _resident((1, D_GMLP)),
                  _resident(w["lbf"].shape), _resident(w["lbb"].shape)],
        out_specs=[tok(TOK_PRE, D_MODEL), tok(TOK_PRE, 2 * D_HGRN), tok(TOK_PRE, 2 * D_HGRN),
                   tok(TOK_PRE, D_HGRN), tok(TOK_PRE, 2 * D_HGRN)],
        out_shape=[jax.ShapeDtypeStruct((n_tok, D_MODEL), F32), pair(BF16), pair(BF16),
                   jax.ShapeDtypeStruct((n_tok, D_HGRN), BF16), pair(F32)],
        scratch_shapes=[pltpu.VMEM((TOK_PRE, D_MODEL), BF16), pltpu.VMEM((TOK_PRE, D_FF), BF16)],
        compiler_params=_params(("parallel",)),
        name="pre",
    )(xf, w["n1"], w["wg1"], w["wu1"], w["wd1"], w["nmix"], w["win"], w["lng"], w["lnb"],
      w["lbf"], w["lbb"])
    x1, gg, qi, g, f = pre_out

    n_blk = seq // TOK_MIX
    seq3 = lambda a: a.reshape(batch, seq, a.shape[-1])
    fwd = pl.BlockSpec((MIX_SEQS, TOK_MIX, D_HGRN), lambda b, c: (b, c, 0))
    bwd = pl.BlockSpec((MIX_SEQS, TOK_MIX, D_HGRN), lambda b, c: (b, n_blk - 1 - c, 0))
    fwd_pair = pl.BlockSpec((MIX_SEQS, TOK_MIX, 2 * D_HGRN), lambda b, c: (b, c, 0))
    bwd_pair = pl.BlockSpec((MIX_SEQS, TOK_MIX, 2 * D_HGRN), lambda b, c: (b, n_blk - 1 - c, 0))
    bwd_hi = pl.BlockSpec((MIX_SEQS, TOK_MIX, D_HGRN), lambda b, c: (b, n_blk - 1 - c, 1))
    seq_shape = lambda dt: jax.ShapeDtypeStruct((batch, seq, D_HGRN), dt)

    def mixer(bounded):
        dfw, dbw = _decay_matrices(bounded)
        return pl.pallas_call(
            functools.partial(_mixer_kernel, bounded=bounded),
            grid=(batch // MIX_SEQS, n_blk),
            in_specs=[fwd_pair, fwd_pair, fwd, bwd_pair, bwd_hi,
                      _resident(w["ws"].shape), _resident(w["bs"].shape), _resident((1, D_GMLP)),
                      _resident(dfw.shape), _resident(dbw.shape)],
            out_specs=[fwd, fwd, bwd],
            out_shape=[seq_shape(BF16), seq_shape(F32), seq_shape(F32)],
            scratch_shapes=[pltpu.VMEM((MIX_SEQS, N_HEADS, HEAD, HEAD), F32),
                            pltpu.VMEM((MIX_SEQS, N_HEADS, HEAD, HEAD), F32)],
            compiler_params=_params(("parallel", "arbitrary")),
            name="mixer_bounded" if bounded else "mixer",
        )(seq3(gg), seq3(qi), seq3(f), seq3(qi), seq3(f), w["ws"], w["bs"], w["gon"], dfw, dbw)

    yg, of, ob = lax.cond(w["lb_min"] >= np.exp(-MAX_EXP_ARG / HALF),
                          lambda: mixer(False), lambda: mixer(True))

    flat = lambda a: a.reshape(n_tok, D_HGRN)
    out = pl.pallas_call(
        _post_kernel,
        grid=(n_tok // TOK_POST,),
        in_specs=[tok(TOK_POST, D_MODEL)] + [tok(TOK_POST, D_HGRN)] * 4
                 + [_resident((1, D_HGRN)), _resident((D_MODEL, D_MODEL)), _resident((1, D_MODEL)),
                    _resident((D_MODEL, D_FF)), _resident((D_MODEL, D_FF)), _resident((D_FF, D_MODEL)),
                    _resident((1, D_MODEL))],
        out_specs=tok(TOK_POST, D_MODEL),
        out_shape=jax.ShapeDtypeStruct((n_tok, D_MODEL), F32),
        scratch_shapes=[pltpu.VMEM((TOK_POST, D_MODEL), BF16), pltpu.VMEM((TOK_POST, D_FF), BF16)],
        compiler_params=_params(("parallel",)),
        name="post",
    )(x1, flat(yg), flat(of), flat(ob), g, w["hon"], w["wout"], w["n2"],
      w["wg2"], w["wu2"], w["wd2"], w["nf"])
    return out.reshape(batch, seq, D_MODEL)


def kernel(x_prompt, x_sample, ffn1_norm, ffn1_w_gate, ffn1_w_up, ffn1_w_down, mix_norm, w_in,
           gmlp_ln_g, gmlp_ln_b, gmlp_w_s, gmlp_b_s, gmlp_out_norm,
           hgrn_lb_fwd, hgrn_lb_bwd, hgrn_out_norm, w_out,
           ffn2_norm, ffn2_w_gate, ffn2_w_up, ffn2_w_down, final_norm):
    lb_of = lambda p: jnp.cumsum(jax.nn.softmax(p.astype(F32), axis=0), axis=0)[0]
    w = dict(
        lb_min=jnp.minimum(jnp.min(lb_of(hgrn_lb_fwd)), jnp.min(lb_of(hgrn_lb_bwd))),
        n1=_row(ffn1_norm[0]), wg1=ffn1_w_gate[0].astype(BF16), wu1=ffn1_w_up[0].astype(BF16),
        wd1=ffn1_w_down[0].astype(BF16), nmix=_row(mix_norm[0]), win=w_in[0].astype(BF16),
        lng=_row(gmlp_ln_g[0]), lnb=_row(gmlp_ln_b[0]),
        lbf=hgrn_lb_fwd.astype(F32), lbb=hgrn_lb_bwd.astype(F32),
        ws=gmlp_w_s[0].astype(BF16), bs=gmlp_b_s[0].T.astype(F32), gon=_row(gmlp_out_norm[0]),
        hon=_row(hgrn_out_norm[0]), wout=w_out[0].astype(BF16),
        n2=_row(ffn2_norm[0]), wg2=ffn2_w_gate[0].astype(BF16), wu2=ffn2_w_up[0].astype(BF16),
        wd2=ffn2_w_down[0].astype(BF16), nf=_row(final_norm),
    )
    return _trunk(x_prompt, w), _trunk(x_sample, w)
```

```python
import functools

import jax
import jax.numpy as jnp
import numpy as np
from jax import lax
from jax.experimental import pallas as pl
from jax.experimental.pallas import tpu as pltpu

F32 = jnp.float32
BF16 = jnp.bfloat16

D_MODEL = 1024
D_GMLP = 512
D_HGRN = 512
N_HEADS = 4
HEAD = 128
CHUNK = 128
HALF = CHUNK // 2
D_FF = 2816
D_IN = 2 * D_GMLP + 5 * D_HGRN
EPS = 1e-6

FF_COLS = 256
FF_STEPS = D_FF // FF_COLS
TOK_PRE = 512
TOK_POST = 512
TOK_MIX = 512
MIX_CHUNKS = TOK_MIX // CHUNK
MIX_SEQS = 1
N_LEVELS = 7
MAX_EXP_ARG = 80.0
VMEM_LIMIT = 56 * 1024 * 1024

SQRT_HALF = float(np.sqrt(0.5).astype(np.float32))

NT_DIMS = (((1,), (1,)), ((), ()))
TN_DIMS = (((0,), (0,)), ((), ()))


def _rms(x, gain):
    return x * lax.rsqrt(jnp.mean(x * x, axis=-1, keepdims=True) + EPS) * gain


def _silu(x):
    return x * jax.nn.sigmoid(x)


def _gelu(x):
    return 0.5 * x * (1.0 + lax.erf(x * SQRT_HALF))


def _swiglu(act_ref, hb_ref, wg_ref, wu_ref, wd_ref):
    for c in range(FF_STEPS):
        cols = slice(c * FF_COLS, (c + 1) * FF_COLS)
        g = jnp.dot(hb_ref[...], wg_ref[:, cols], preferred_element_type=F32)
        u = jnp.dot(hb_ref[...], wu_ref[:, cols], preferred_element_type=F32)
        act_ref[:, cols] = (_silu(g) * u).astype(BF16)
    return jnp.dot(act_ref[...], wd_ref[...], preferred_element_type=F32)


def _lower_bound(p):
    e = jnp.exp(p - jnp.max(p, axis=0, keepdims=True))
    return e[0:1, :] / jnp.sum(e, axis=0, keepdims=True)


def _pre_kernel(x_ref, n1_ref, wg_ref, wu_ref, wd_ref, nmix_ref, win_ref, lng_ref, lnb_ref,
                lbf_ref, lbb_ref,
                x1_ref, gu_ref, gv_ref, qs_ref, i_ref, ff_ref, fb_ref, g_ref,
                hb_ref, act_ref):
    x = x_ref[...]
    hb_ref[...] = _rms(x, n1_ref[...]).astype(BF16)
    x1 = x + 0.5 * _swiglu(act_ref, hb_ref, wg_ref, wu_ref, wd_ref)
    x1_ref[...] = x1
    hb_ref[...] = _rms(x1, nmix_ref[...]).astype(BF16)

    def proj(j):
        return jnp.dot(hb_ref[...], win_ref[:, j * D_HGRN:(j + 1) * D_HGRN],
                       preferred_element_type=F32)

    gu_ref[...] = _gelu(proj(0)).astype(BF16)
    v = _gelu(proj(1))
    mu = jnp.mean(v, axis=-1, keepdims=True)
    vc = v - mu
    gv_ref[...] = (vc * lax.rsqrt(jnp.mean(vc * vc, axis=-1, keepdims=True) + EPS)
                   * lng_ref[...] + lnb_ref[...]).astype(BF16)
    qs_ref[...] = _silu(proj(2)).astype(BF16)
    i_ref[...] = proj(3).astype(BF16)
    lbf = _lower_bound(lbf_ref[...])
    ff_ref[...] = lbf + (1.0 - lbf) * jax.nn.sigmoid(proj(4))
    lbb = _lower_bound(lbb_ref[...])
    fb_ref[...] = lbb + (1.0 - lbb) * jax.nn.sigmoid(proj(5))
    g_ref[...] = proj(6).astype(BF16)


def _heads():
    return [slice(h * HEAD, (h + 1) * HEAD) for h in range(N_HEADS)]


def _log_decay(f, dm_ref):
    lf = jnp.log(f)
    hi = lf.astype(BF16)
    lo = (lf - hi.astype(F32)).astype(BF16)
    hilo = jnp.concatenate([hi, lo], axis=0)
    return lf, 1.0 - f, lambda i: jnp.dot(dm_ref[i], hilo, preferred_element_type=F32)


def _single_ref_prep(f, dm_ref, forward):
    lf, k, cum = _log_decay(f, dm_ref)
    sum_a = jnp.sum(lf[:HALF], axis=0, keepdims=True)
    sum_b = jnp.sum(lf[HALF:], axis=0, keepdims=True)
    r_in, r_out = (sum_a, sum_b) if forward else (sum_b, sum_a)
    return k, cum(0), r_in, r_out


def _single_ref_scores(qs, vb, prep, forward):
    k, x, r_in, r_out = prep
    row = lax.broadcasted_iota(jnp.int32, (CHUNK, CHUNK), 0)
    col = lax.broadcasted_iota(jnp.int32, (CHUNK, CHUNK), 1)
    qt = qs * jnp.exp(x)
    kt = k * jnp.exp(-x)
    qh = (qt * jnp.exp(r_in)).astype(BF16)
    kh = (kt * jnp.exp(r_out)).astype(BF16)
    dec = jnp.exp(r_in + r_out)
    qt = qt.astype(BF16)
    kt = kt.astype(BF16)
    visible = (col <= row) if forward else (col >= row)
    scores = [jnp.where(visible,
                        lax.dot_general(qt[:, sl], kt[:, sl], NT_DIMS, preferred_element_type=F32), 0.0)
              for sl in _heads()]
    ds = [lax.dot_general(vb[:, sl], kh[:, sl], TN_DIMS, preferred_element_type=F32) for sl in _heads()]
    return scores, qh, ds, dec, None


def _bounded_scores(qs, v, f, dm_ref, forward):
    lf, k, cum = _log_decay(f, dm_ref)
    row = lax.broadcasted_iota(jnp.int32, (CHUNK, CHUNK), 0)
    col = lax.broadcasted_iota(jnp.int32, (CHUNK, CHUNK), 1)
    tok = lax.broadcasted_iota(jnp.int32, (CHUNK, D_HGRN), 0)
    qh = (qs * jnp.exp(cum(N_LEVELS))).astype(BF16)
    kh = (k * jnp.exp(cum(N_LEVELS + 1))).astype(BF16)
    dec = jnp.exp(jnp.sum(lf, axis=0, keepdims=True))
    scores = [jnp.zeros((CHUNK, CHUNK), F32)] * N_HEADS
    for lvl in range(N_LEVELS):
        m = 1 << lvl
        is_query = ((tok & m) != 0) if forward else ((tok & m) == 0)
        z = (jnp.where(is_query, qs, k) * jnp.exp(-jnp.abs(cum(lvl)))).astype(BF16)
        pair = (((row ^ col) >> lvl) == 1) & ((col < row) if forward else (col > row))
        scores = [sc + jnp.where(pair, lax.dot_general(z[:, sl], z[:, sl], NT_DIMS,
                                                       preferred_element_type=F32), 0.0)
                  for sc, sl in zip(scores, _heads())]
    same_pos = jnp.concatenate(
        [jnp.sum(qs[:, sl] * k[:, sl], axis=-1, keepdims=True) * v[:, sl] for sl in _heads()], axis=-1)
    vb = v.astype(BF16)
    ds = [lax.dot_general(vb[:, sl], kh[:, sl], TN_DIMS, preferred_element_type=F32) for sl in _heads()]
    return scores, qh, ds, dec, same_pos


def _mixer_kernel(gu_ref, gv_ref, qsf_ref, if_ref, ff_ref, qsb_ref, ib_ref, fb_ref,
                  ws_ref, bs_ref, gon_ref, dfw_ref, dbw_ref,
                  yg_ref, of_ref, ob_ref,
                  sf_ref, sb_ref, *, bounded):
    @pl.when(pl.program_id(1) == 0)
    def _():
        sf_ref[...] = jnp.zeros_like(sf_ref)
        sb_ref[...] = jnp.zeros_like(sb_ref)

    chunk_rows = [slice(c * CHUNK, (c + 1) * CHUNK) for c in range(MIX_CHUNKS)]
    streams = ((qsf_ref, if_ref, ff_ref, dfw_ref, True, of_ref, sf_ref, chunk_rows),
               (qsb_ref, ib_ref, fb_ref, dbw_ref, False, ob_ref, sb_ref, chunk_rows[::-1]))
    items = [(s, n) for s in range(MIX_SEQS) for n in range(MIX_CHUNKS)]

    prep = {}
    for s, n in items:
        rows = chunk_rows[n]
        gv = gv_ref[s, rows, :]
        mixed = [jnp.dot(ws_ref[h], gv[:, sl], preferred_element_type=F32) + bs_ref[:, h:h + 1]
                 for h, sl in enumerate(_heads())]
        yg_ref[s, rows, :] = _rms(gu_ref[s, rows, :] * jnp.concatenate(mixed, axis=-1),
                                  gon_ref[...]).astype(BF16)
        if not bounded:
            for d, (_, _, f_ref, dm_ref, fw, _, _, order) in enumerate(streams):
                prep[s, n, d] = _single_ref_prep(f_ref[s, order[n], :], dm_ref, fw)

    mid = {}
    for s, n in items:
        for d, (qs_ref, v_ref, f_ref, dm_ref, fw, _, _, order) in enumerate(streams):
            rows = order[n]
            if bounded:
                mid[s, n, d] = _bounded_scores(qs_ref[s, rows, :], v_ref[s, rows, :], f_ref[s, rows, :],
                                               dm_ref, fw)
            else:
                mid[s, n, d] = _single_ref_scores(qs_ref[s, rows, :], v_ref[s, rows, :],
                                                  prep[s, n, d], fw)

    state = {(s, d): [st_ref[s, h] for h in range(N_HEADS)]
             for s in range(MIX_SEQS) for d, (*_, st_ref, _) in enumerate(streams)}
    for n in range(MIX_CHUNKS):
        for s in range(MIX_SEQS):
            for d, (_, v_ref, _, _, _, o_ref, _, order) in enumerate(streams):
                rows = order[n]
                scores, qh, ds, dec, same_pos = mid[s, n, d]
                vb = v_ref[s, rows, :]
                outs = []
                for h, sl in enumerate(_heads()):
                    st = state[s, d][h]
                    o = jnp.dot(scores[h].astype(BF16), vb[:, sl], preferred_element_type=F32)
                    o += lax.dot_general(qh[:, sl], st.astype(BF16), NT_DIMS,
                                         preferred_element_type=F32)
                    outs.append(o)
                    state[s, d][h] = st * dec[:, sl] + ds[h]
                o = jnp.concatenate(outs, axis=-1)
                o_ref[s, rows, :] = o if same_pos is None else o + same_pos
    for (s, d), st in state.items():
        for h in range(N_HEADS):
            streams[d][6][s, h] = st[h]


def _post_kernel(x1_ref, yg_ref, of_ref, ob_ref, g_ref, hon_ref, wout_ref, n2_ref,
                 wg_ref, wu_ref, wd_ref, nf_ref,
                 out_ref,
                 hb_ref, act_ref):
    o = of_ref[...] + ob_ref[...]
    heads = []
    for h in range(N_HEADS):
        sl = slice(h * HEAD, (h + 1) * HEAD)
        heads.append(_rms(o[:, sl], hon_ref[:, sl]))
    yh = jnp.concatenate(heads, axis=-1) * _silu(g_ref[...].astype(F32))
    x2 = (x1_ref[...]
          + jnp.dot(yg_ref[...], wout_ref[:D_GMLP, :], preferred_element_type=F32)
          + jnp.dot(yh.astype(BF16), wout_ref[D_GMLP:, :], preferred_element_type=F32))
    hb_ref[...] = _rms(x2, n2_ref[...]).astype(BF16)
    x3 = x2 + 0.5 * _swiglu(act_ref, hb_ref, wg_ref, wu_ref, wd_ref)
    out_ref[...] = _rms(x3, nf_ref[...])


def _resident(shape):
    nd = len(shape)
    return pl.BlockSpec(shape, lambda *_: (0,) * nd, pipeline_mode=pl.Buffered(1))


def _decay_matrices(bounded):
    t = np.arange(CHUNK)
    out = []
    for forward in (True, False):
        cums = (t[None, :] <= t[:, None]) if forward else (t[None, :] >= t[:, None])
        cums = cums.astype(np.float32)
        if bounded:
            mats = []
            for lvl in range(N_LEVELS):
                m = 1 << lvl
                ref = (t // (2 * m)) * (2 * m) + (m - 1 if forward else m)
                mats.append(cums - cums[ref])
            mats += [cums, 1.0 - cums]
        else:
            mats = [cums - cums[HALF - 1 if forward else HALF]]
        d = np.stack(mats)
        out.append(jnp.asarray(np.concatenate([d, d], axis=2), dtype=BF16))
    return out


def _row(v):
    return v.reshape(1, -1).astype(F32)


def _params(sem):
    return pltpu.CompilerParams(dimension_semantics=sem, vmem_limit_bytes=VMEM_LIMIT)


def _trunk(x, w):
    batch, seq, _ = x.shape
    n_tok = batch * seq
    assert n_tok % TOK_PRE == 0 and n_tok % TOK_POST == 0 and seq % TOK_MIX == 0
    assert batch % MIX_SEQS == 0
    xf = x.reshape(n_tok, D_MODEL)

    tok = lambda t, d: pl.BlockSpec((t, d), lambda i: (i, 0))
    half = lambda dt: jax.ShapeDtypeStruct((n_tok, D_HGRN), dt)

    pre_out = pl.pallas_call(
        _pre_kernel,
        grid=(n_tok // TOK_PRE,),
        in_specs=[tok(TOK_PRE, D_MODEL), _resident((1, D_MODEL)),
                  _resident((D_MODEL, D_FF)), _resident((D_MODEL, D_FF)), _resident((D_FF, D_MODEL)),
                  _resident((1, D_MODEL)), _resident((D_MODEL, D_IN)),
                  _resident((1, D_GMLP)), _resident((1, D_GMLP)),
                  _resident(w["lbf"].shape), _resident(w["lbb"].shape)],
        out_specs=[tok(TOK_PRE, D_MODEL)] + [tok(TOK_PRE, D_HGRN)] * 7,
        out_shape=[jax.ShapeDtypeStruct((n_tok, D_MODEL), F32)]
                  + [half(dt) for dt in (BF16, BF16, BF16, BF16, F32, F32, BF16)],
        scratch_shapes=[pltpu.VMEM((TOK_PRE, D_MODEL), BF16), pltpu.VMEM((TOK_PRE, D_FF), BF16)],
        compiler_params=_params(("parallel",)),
        name="pre",
    )(xf, w["n1"], w["wg1"], w["wu1"], w["wd1"], w["nmix"], w["win"], w["lng"], w["lnb"],
      w["lbf"], w["lbb"])
    x1, gu, gv, qs, iv, ff, fb, g = pre_out
    yield

    n_blk = seq // TOK_MIX
    seq3 = lambda a: a.reshape(batch, seq, D_HGRN)
    fwd = pl.BlockSpec((MIX_SEQS, TOK_MIX, D_HGRN), lambda b, c: (b, c, 0))
    bwd = pl.BlockSpec((MIX_SEQS, TOK_MIX, D_HGRN), lambda b, c: (b, n_blk - 1 - c, 0))
    seq_shape = lambda dt: jax.ShapeDtypeStruct((batch, seq, D_HGRN), dt)

    def mixer(bounded):
        dfw, dbw = _decay_matrices(bounded)
        return pl.pallas_call(
            functools.partial(_mixer_kernel, bounded=bounded),
            grid=(batch // MIX_SEQS, n_blk),
            in_specs=[fwd, fwd, fwd, fwd, fwd, bwd, bwd, bwd,
                      _resident(w["ws"].shape), _resident(w["bs"].shape), _resident((1, D_GMLP)),
                      _resident(dfw.shape), _resident(dbw.shape)],
            out_specs=[fwd, fwd, bwd],
            out_shape=[seq_shape(BF16), seq_shape(F32), seq_shape(F32)],
            scratch_shapes=[pltpu.VMEM((MIX_SEQS, N_HEADS, HEAD, HEAD), F32),
                            pltpu.VMEM((MIX_SEQS, N_HEADS, HEAD, HEAD), F32)],
            compiler_params=_params(("parallel", "arbitrary")),
            name="mixer_bounded" if bounded else "mixer",
        )(seq3(gu), seq3(gv), seq3(qs), seq3(iv), seq3(ff), seq3(qs), seq3(iv), seq3(fb),
          w["ws"], w["bs"], w["gon"], dfw, dbw)

    yg, of, ob = lax.cond(w["lb_min"] >= np.exp(-MAX_EXP_ARG / HALF),
                          lambda: mixer(False), lambda: mixer(True))
    yield

    flat = lambda a: a.reshape(n_tok, D_HGRN)
    out = pl.pallas_call(
        _post_kernel,
        grid=(n_tok // TOK_POST,),
        in_specs=[tok(TOK_POST, D_MODEL)] + [tok(TOK_POST, D_HGRN)] * 4
                 + [_resident((1, D_HGRN)), _resident((D_MODEL, D_MODEL)), _resident((1, D_MODEL)),
                    _resident((D_MODEL, D_FF)), _resident((D_MODEL, D_FF)), _resident((D_FF, D_MODEL)),
                    _resident((1, D_MODEL))],
        out_specs=tok(TOK_POST, D_MODEL),
        out_shape=jax.ShapeDtypeStruct((n_tok, D_MODEL), F32),
        scratch_shapes=[pltpu.VMEM((TOK_POST, D_MODEL), BF16), pltpu.VMEM((TOK_POST, D_FF), BF16)],
        compiler_params=_params(("parallel",)),
        name="post",
    )(x1, flat(yg), flat(of), flat(ob), g, w["hon"], w["wout"], w["n2"],
      w["wg2"], w["wu2"], w["wd2"], w["nf"])
    yield out.reshape(batch, seq, D_MODEL)


def kernel(x_prompt, x_sample, ffn1_norm, ffn1_w_gate, ffn1_w_up, ffn1_w_down, mix_norm, w_in,
           gmlp_ln_g, gmlp_ln_b, gmlp_w_s, gmlp_b_s, gmlp_out_norm,
           hgrn_lb_fwd, hgrn_lb_bwd, hgrn_out_norm, w_out,
           ffn2_norm, ffn2_w_gate, ffn2_w_up, ffn2_w_down, final_norm):
    lb_of = lambda p: jnp.cumsum(jax.nn.softmax(p.astype(F32), axis=0), axis=0)[0]
    w = dict(
        lb_min=jnp.minimum(jnp.min(lb_of(hgrn_lb_fwd)), jnp.min(lb_of(hgrn_lb_bwd))),
        n1=_row(ffn1_norm[0]), wg1=ffn1_w_gate[0].astype(BF16), wu1=ffn1_w_up[0].astype(BF16),
        wd1=ffn1_w_down[0].astype(BF16), nmix=_row(mix_norm[0]), win=w_in[0].astype(BF16),
        lng=_row(gmlp_ln_g[0]), lnb=_row(gmlp_ln_b[0]),
        lbf=hgrn_lb_fwd.astype(F32), lbb=hgrn_lb_bwd.astype(F32),
        ws=gmlp_w_s[0].astype(BF16), bs=gmlp_b_s[0].T.astype(F32), gon=_row(gmlp_out_norm[0]),
        hon=_row(hgrn_out_norm[0]), wout=w_out[0].astype(BF16),
        n2=_row(ffn2_norm[0]), wg2=ffn2_w_gate[0].astype(BF16), wu2=ffn2_w_up[0].astype(BF16),
        wd2=ffn2_w_down[0].astype(BF16), nf=_row(final_norm),
    )
    trunks = (_trunk(x_prompt, w), _trunk(x_sample, w))
    for _ in range(2):
        for t in trunks:
            next(t)
    return tuple(next(t) for t in trunks)
```

```python
import functools

import jax
import jax.numpy as jnp
import numpy as np
from jax import lax
from jax.experimental import pallas as pl
from jax.experimental.pallas import tpu as pltpu

F32 = jnp.float32
BF16 = jnp.bfloat16

D_MODEL = 1024
D_GMLP = 512
D_HGRN = 512
N_HEADS = 4
HEAD = 128
CHUNK = 128
HALF = CHUNK // 2
D_FF = 2816
D_IN = 2 * D_GMLP + 5 * D_HGRN
EPS = 1e-6

FF_COLS = 256
FF_STEPS = D_FF // FF_COLS
TOK_PRE = 512
TOK_POST = 512
TOK_MIX = 512
MIX_CHUNKS = TOK_MIX // CHUNK
MIX_SEQS = 1
CAST_STEPS = 8
BF16_ROWS = 16
N_LEVELS = 7
MAX_EXP_ARG = 80.0
VMEM_LIMIT = 56 * 1024 * 1024

SQRT_HALF = float(np.sqrt(0.5).astype(np.float32))

NT_DIMS = (((1,), (1,)), ((), ()))
TN_DIMS = (((0,), (0,)), ((), ()))


def _rms(x, gain):
    return x * lax.rsqrt(jnp.mean(x * x, axis=-1, keepdims=True) + EPS) * gain


def _silu(x):
    return x * jax.nn.sigmoid(x)


def _gelu(x):
    return 0.5 * x * (1.0 + lax.erf(x * SQRT_HALF))


def _swiglu(act_ref, hb_ref, wg_ref, wu_ref, wd_ref):
    for c in range(FF_STEPS):
        cols = slice(c * FF_COLS, (c + 1) * FF_COLS)
        g = jnp.dot(hb_ref[...], wg_ref[:, cols], preferred_element_type=F32)
        u = jnp.dot(hb_ref[...], wu_ref[:, cols], preferred_element_type=F32)
        act_ref[:, cols] = (_silu(g) * u).astype(BF16)
    return jnp.dot(act_ref[...], wd_ref[...], preferred_element_type=F32)


def _lower_bound(p):
    e = jnp.exp(p - jnp.max(p, axis=0, keepdims=True))
    return e[0:1, :] / jnp.sum(e, axis=0, keepdims=True)


def _pre_kernel(x_ref, n1_ref, wg_ref, wu_ref, wd_ref, nmix_ref, win_ref, lng_ref, lnb_ref,
                lbf_ref, lbb_ref,
                x1_ref, gu_ref, gv_ref, qs_ref, i_ref, ff_ref, fb_ref, g_ref,
                hb_ref, act_ref):
    x = x_ref[...]
    hb_ref[...] = _rms(x, n1_ref[...]).astype(BF16)
    x1 = x + 0.5 * _swiglu(act_ref, hb_ref, wg_ref, wu_ref, wd_ref)
    x1_ref[...] = x1
    hb_ref[...] = _rms(x1, nmix_ref[...]).astype(BF16)

    def proj(j):
        return jnp.dot(hb_ref[...], win_ref[:, j * D_HGRN:(j + 1) * D_HGRN],
                       preferred_element_type=F32)

    gu_ref[...] = _gelu(proj(0)).astype(BF16)
    v = _gelu(proj(1))
    mu = jnp.mean(v, axis=-1, keepdims=True)
    vc = v - mu
    gv_ref[...] = (vc * lax.rsqrt(jnp.mean(vc * vc, axis=-1, keepdims=True) + EPS)
                   * lng_ref[...] + lnb_ref[...]).astype(BF16)
    qs_ref[...] = _silu(proj(2)).astype(BF16)
    i_ref[...] = proj(3).astype(BF16)
    lbf = _lower_bound(lbf_ref[...])
    ff_ref[...] = lbf + (1.0 - lbf) * jax.nn.sigmoid(proj(4))
    lbb = _lower_bound(lbb_ref[...])
    fb_ref[...] = lbb + (1.0 - lbb) * jax.nn.sigmoid(proj(5))
    g_ref[...] = proj(6).astype(BF16)


def _heads():
    return [slice(h * HEAD, (h + 1) * HEAD) for h in range(N_HEADS)]


def _log_decay(f, dm_ref):
    lf = jnp.log(f)
    hi = lf.astype(BF16)
    lo = (lf - hi.astype(F32)).astype(BF16)
    hilo = jnp.concatenate([hi, lo], axis=0)
    return lf, 1.0 - f, lambda i: jnp.dot(dm_ref[i], hilo, preferred_element_type=F32)


def _single_ref_prep(f, dm_ref, forward):
    lf, k, cum = _log_decay(f, dm_ref)
    sum_a = jnp.sum(lf[:HALF], axis=0, keepdims=True)
    sum_b = jnp.sum(lf[HALF:], axis=0, keepdims=True)
    r_in, r_out = (sum_a, sum_b) if forward else (sum_b, sum_a)
    return k, cum(0), r_in, r_out


def _single_ref_scores(qs, vb, prep, forward):
    k, x, r_in, r_out = prep
    row = lax.broadcasted_iota(jnp.int32, (CHUNK, CHUNK), 0)
    col = lax.broadcasted_iota(jnp.int32, (CHUNK, CHUNK), 1)
    qt = qs * jnp.exp(x)
    kt = k * jnp.exp(-x)
    qh = (qt * jnp.exp(r_in)).astype(BF16)
    kh = (kt * jnp.exp(r_out)).astype(BF16)
    dec = jnp.exp(r_in + r_out)
    qt = qt.astype(BF16)
    kt = kt.astype(BF16)
    visible = (col <= row) if forward else (col >= row)
    scores = [jnp.where(visible,
                        lax.dot_general(qt[:, sl], kt[:, sl], NT_DIMS, preferred_element_type=F32), 0.0)
              for sl in _heads()]
    ds = [lax.dot_general(vb[:, sl], kh[:, sl], TN_DIMS, preferred_element_type=F32) for sl in _heads()]
    return scores, qh, ds, dec, None


def _bounded_scores(qs, v, f, dm_ref, forward):
    lf, k, cum = _log_decay(f, dm_ref)
    row = lax.broadcasted_iota(jnp.int32, (CHUNK, CHUNK), 0)
    col = lax.broadcasted_iota(jnp.int32, (CHUNK, CHUNK), 1)
    tok = lax.broadcasted_iota(jnp.int32, (CHUNK, D_HGRN), 0)
    qh = (qs * jnp.exp(cum(N_LEVELS))).astype(BF16)
    kh = (k * jnp.exp(cum(N_LEVELS + 1))).astype(BF16)
    dec = jnp.exp(jnp.sum(lf, axis=0, keepdims=True))
    scores = [jnp.zeros((CHUNK, CHUNK), F32)] * N_HEADS
    for lvl in range(N_LEVELS):
        m = 1 << lvl
        is_query = ((tok & m) != 0) if forward else ((tok & m) == 0)
        z = (jnp.where(is_query, qs, k) * jnp.exp(-jnp.abs(cum(lvl)))).astype(BF16)
        pair = (((row ^ col) >> lvl) == 1) & ((col < row) if forward else (col > row))
        scores = [sc + jnp.where(pair, lax.dot_general(z[:, sl], z[:, sl], NT_DIMS,
                                                       preferred_element_type=F32), 0.0)
                  for sc, sl in zip(scores, _heads())]
    same_pos = jnp.concatenate(
        [jnp.sum(qs[:, sl] * k[:, sl], axis=-1, keepdims=True) * v[:, sl] for sl in _heads()], axis=-1)
    vb = v.astype(BF16)
    ds = [lax.dot_general(vb[:, sl], kh[:, sl], TN_DIMS, preferred_element_type=F32) for sl in _heads()]
    return scores, qh, ds, dec, same_pos


def _mixer_kernel(gu_ref, gv_ref, qsf_ref, if_ref, ff_ref, qsb_ref, ib_ref, fb_ref,
                  ws_ref, bs_ref, gon_ref, dfw_ref, dbw_ref,
                  yg_ref, of_ref, ob_ref,
                  sf_ref, sb_ref, *, bounded):
    @pl.when(pl.program_id(1) == 0)
    def _():
        sf_ref[...] = jnp.zeros_like(sf_ref)
        sb_ref[...] = jnp.zeros_like(sb_ref)

    chunk_rows = [slice(c * CHUNK, (c + 1) * CHUNK) for c in range(MIX_CHUNKS)]
    streams = ((qsf_ref, if_ref, ff_ref, dfw_ref, True, of_ref, sf_ref, chunk_rows),
               (qsb_ref, ib_ref, fb_ref, dbw_ref, False, ob_ref, sb_ref, chunk_rows[::-1]))
    items = [(s, n) for s in range(MIX_SEQS) for n in range(MIX_CHUNKS)]

    prep = {}
    for s, n in items:
        rows = chunk_rows[n]
        gv = gv_ref[s, rows, :]
        mixed = [jnp.dot(ws_ref[h], gv[:, sl], preferred_element_type=F32) + bs_ref[:, h:h + 1]
                 for h, sl in enumerate(_heads())]
        yg_ref[s, rows, :] = _rms(gu_ref[s, rows, :] * jnp.concatenate(mixed, axis=-1),
                                  gon_ref[...]).astype(BF16)
        if not bounded:
            for d, (_, _, f_ref, dm_ref, fw, _, _, order) in enumerate(streams):
                prep[s, n, d] = _single_ref_prep(f_ref[s, order[n], :], dm_ref, fw)

    mid = {}
    for s, n in items:
        for d, (qs_ref, v_ref, f_ref, dm_ref, fw, _, _, order) in enumerate(streams):
            rows = order[n]
            if bounded:
                mid[s, n, d] = _bounded_scores(qs_ref[s, rows, :], v_ref[s, rows, :], f_ref[s, rows, :],
                                               dm_ref, fw)
            else:
                mid[s, n, d] = _single_ref_scores(qs_ref[s, rows, :], v_ref[s, rows, :],
                                                  prep[s, n, d], fw)

    state = {(s, d): [st_ref[s, h] for h in range(N_HEADS)]
             for s in range(MIX_SEQS) for d, (*_, st_ref, _) in enumerate(streams)}
    for n in range(MIX_CHUNKS):
        for s in range(MIX_SEQS):
            for d, (_, v_ref, _, _, _, o_ref, _, order) in enumerate(streams):
                rows = order[n]
                scores, qh, ds, dec, same_pos = mid[s, n, d]
                vb = v_ref[s, rows, :]
                outs = []
                for h, sl in enumerate(_heads()):
                    st = state[s, d][h]
                    o = jnp.dot(scores[h].astype(BF16), vb[:, sl], preferred_element_type=F32)
                    o += lax.dot_general(qh[:, sl], st.astype(BF16), NT_DIMS,
                                         preferred_element_type=F32)
                    outs.append(o)
                    state[s, d][h] = st * dec[:, sl] + ds[h]
                o = jnp.concatenate(outs, axis=-1)
                o_ref[s, rows, :] = o if same_pos is None else o + same_pos
    for (s, d), st in state.items():
        for h in range(N_HEADS):
            streams[d][6][s, h] = st[h]


def _post_kernel(x1_ref, yg_ref, of_ref, ob_ref, g_ref, hon_ref, wout_ref, n2_ref,
                 wg_ref, wu_ref, wd_ref, nf_ref,
                 out_ref,
                 hb_ref, act_ref):
    o = of_ref[...] + ob_ref[...]
    heads = []
    for h in range(N_HEADS):
        sl = slice(h * HEAD, (h + 1) * HEAD)
        heads.append(_rms(o[:, sl], hon_ref[:, sl]))
    yh = jnp.concatenate(heads, axis=-1) * _silu(g_ref[...].astype(F32))
    x2 = (x1_ref[...]
          + jnp.dot(yg_ref[...], wout_ref[:D_GMLP, :], preferred_element_type=F32)
          + jnp.dot(yh.astype(BF16), wout_ref[D_GMLP:, :], preferred_element_type=F32))
    hb_ref[...] = _rms(x2, n2_ref[...]).astype(BF16)
    x3 = x2 + 0.5 * _swiglu(act_ref, hb_ref, wg_ref, wu_ref, wd_ref)
    out_ref[...] = _rms(x3, nf_ref[...])


def _resident(shape):
    nd = len(shape)
    return pl.BlockSpec(shape, lambda *_: (0,) * nd, pipeline_mode=pl.Buffered(1))


def _decay_matrices(bounded):
    t = np.arange(CHUNK)
    out = []
    for forward in (True, False):
        cums = (t[None, :] <= t[:, None]) if forward else (t[None, :] >= t[:, None])
        cums = cums.astype(np.float32)
        if bounded:
            mats = []
            for lvl in range(N_LEVELS):
                m = 1 << lvl
                ref = (t // (2 * m)) * (2 * m) + (m - 1 if forward else m)
                mats.append(cums - cums[ref])
            mats += [cums, 1.0 - cums]
        else:
            mats = [cums - cums[HALF - 1 if forward else HALF]]
        d = np.stack(mats)
        out.append(jnp.asarray(np.concatenate([d, d], axis=2), dtype=BF16))
    return out


def _row(v):
    return v.reshape(1, -1).astype(F32)


def _cast_kernel(*refs):
    n = len(refs) // 2
    for src, dst in zip(refs[:n], refs[n:]):
        dst[...] = src[...].astype(BF16)


def _to_bf16(arrays):
    specs = []
    for a in arrays:
        rows = a.shape[0] // CAST_STEPS
        assert a.shape[0] % CAST_STEPS == 0 and rows % BF16_ROWS == 0
        specs.append(pl.BlockSpec((rows, a.shape[1]), lambda i: (i, 0)))
    return pl.pallas_call(
        _cast_kernel,
        grid=(CAST_STEPS,),
        in_specs=specs,
        out_specs=specs,
        out_shape=[jax.ShapeDtypeStruct(a.shape, BF16) for a in arrays],
        compiler_params=_params(("parallel",)),
        name="to_bf16",
    )(*arrays)


def _params(sem):
    return pltpu.CompilerParams(dimension_semantics=sem, vmem_limit_bytes=VMEM_LIMIT)


def _trunk(x, w):
    batch, seq, _ = x.shape
    n_tok = batch * seq
    assert n_tok % TOK_PRE == 0 and n_tok % TOK_POST == 0 and seq % TOK_MIX == 0
    assert batch % MIX_SEQS == 0
    xf = x.reshape(n_tok, D_MODEL)

    tok = lambda t, d: pl.BlockSpec((t, d), lambda i: (i, 0))
    half = lambda dt: jax.ShapeDtypeStruct((n_tok, D_HGRN), dt)

    pre_out = pl.pallas_call(
        _pre_kernel,
        grid=(n_tok // TOK_PRE,),
        in_specs=[tok(TOK_PRE, D_MODEL), _resident((1, D_MODEL)),
                  _resident((D_MODEL, D_FF)), _resident((D_MODEL, D_FF)), _resident((D_FF, D_MODEL)),
                  _resident((1, D_MODEL)), _resident((D_MODEL, D_IN)),
                  _resident((1, D_GMLP)), _resident((1, D_GMLP)),
                  _resident(w["lbf"].shape), _resident(w["lbb"].shape)],
        out_specs=[tok(TOK_PRE, D_MODEL)] + [tok(TOK_PRE, D_HGRN)] * 7,
        out_shape=[jax.ShapeDtypeStruct((n_tok, D_MODEL), F32)]
                  + [half(dt) for dt in (BF16, BF16, BF16, BF16, F32, F32, BF16)],
        scratch_shapes=[pltpu.VMEM((TOK_PRE, D_MODEL), BF16), pltpu.VMEM((TOK_PRE, D_FF), BF16)],
        compiler_params=_params(("parallel",)),
        name="pre",
    )(xf, w["n1"], w["wg1"], w["wu1"], w["wd1"], w["nmix"], w["win"], w["lng"], w["lnb"],
      w["lbf"], w["lbb"])
    x1, gu, gv, qs, iv, ff, fb, g = pre_out

    n_blk = seq // TOK_MIX
    seq3 = lambda a: a.reshape(batch, seq, D_HGRN)
    fwd = pl.BlockSpec((MIX_SEQS, TOK_MIX, D_HGRN), lambda b, c: (b, c, 0))
    bwd = pl.BlockSpec((MIX_SEQS, TOK_MIX, D_HGRN), lambda b, c: (b, n_blk - 1 - c, 0))
    seq_shape = lambda dt: jax.ShapeDtypeStruct((batch, seq, D_HGRN), dt)

    def mixer(bounded):
        dfw, dbw = _decay_matrices(bounded)
        return pl.pallas_call(
            functools.partial(_mixer_kernel, bounded=bounded),
            grid=(batch // MIX_SEQS, n_blk),
            in_specs=[fwd, fwd, fwd, fwd, fwd, bwd, bwd, bwd,
                      _resident(w["ws"].shape), _resident(w["bs"].shape), _resident((1, D_GMLP)),
                      _resident(dfw.shape), _resident(dbw.shape)],
            out_specs=[fwd, fwd, bwd],
            out_shape=[seq_shape(BF16), seq_shape(F32), seq_shape(F32)],
            scratch_shapes=[pltpu.VMEM((MIX_SEQS, N_HEADS, HEAD, HEAD), F32),
                            pltpu.VMEM((MIX_SEQS, N_HEADS, HEAD, HEAD), F32)],
            compiler_params=_params(("parallel", "arbitrary")),
            name="mixer_bounded" if bounded else "mixer",
        )(seq3(gu), seq3(gv), seq3(qs), seq3(iv), seq3(ff), seq3(qs), seq3(iv), seq3(fb),
          w["ws"], w["bs"], w["gon"], dfw, dbw)

    yg, of, ob = lax.cond(w["lb_min"] >= np.exp(-MAX_EXP_ARG / HALF),
                          lambda: mixer(False), lambda: mixer(True))

    flat = lambda a: a.reshape(n_tok, D_HGRN)
    out = pl.pallas_call(
        _post_kernel,
        grid=(n_tok // TOK_POST,),
        in_specs=[tok(TOK_POST, D_MODEL)] + [tok(TOK_POST, D_HGRN)] * 4
                 + [_resident((1, D_HGRN)), _resident((D_MODEL, D_MODEL)), _resident((1, D_MODEL)),
                    _resident((D_MODEL, D_FF)), _resident((D_MODEL, D_FF)), _resident((D_FF, D_MODEL)),
                    _resident((1, D_MODEL))],
        out_specs=tok(TOK_POST, D_MODEL),
        out_shape=jax.ShapeDtypeStruct((n_tok, D_MODEL), F32),
        scratch_shapes=[pltpu.VMEM((TOK_POST, D_MODEL), BF16), pltpu.VMEM((TOK_POST, D_FF), BF16)],
        compiler_params=_params(("parallel",)),
        name="post",
    )(x1, flat(yg), flat(of), flat(ob), g, w["hon"], w["wout"], w["n2"],
      w["wg2"], w["wu2"], w["wd2"], w["nf"])
    return out.reshape(batch, seq, D_MODEL)


def kernel(x_prompt, x_sample, ffn1_norm, ffn1_w_gate, ffn1_w_up, ffn1_w_down, mix_norm, w_in,
           gmlp_ln_g, gmlp_ln_b, gmlp_w_s, gmlp_b_s, gmlp_out_norm,
           hgrn_lb_fwd, hgrn_lb_bwd, hgrn_out_norm, w_out,
           ffn2_norm, ffn2_w_gate, ffn2_w_up, ffn2_w_down, final_norm):
    lb_of = lambda p: jnp.cumsum(jax.nn.softmax(p.astype(F32), axis=0), axis=0)[0]
    wg1, wu1, wd1, win, wout, wg2, wu2, wd2 = _to_bf16(
        [a[0].astype(F32) for a in (ffn1_w_gate, ffn1_w_up, ffn1_w_down, w_in, w_out,
                                    ffn2_w_gate, ffn2_w_up, ffn2_w_down)])
    w = dict(
        lb_min=jnp.minimum(jnp.min(lb_of(hgrn_lb_fwd)), jnp.min(lb_of(hgrn_lb_bwd))),
        n1=_row(ffn1_norm[0]), wg1=wg1, wu1=wu1, wd1=wd1, nmix=_row(mix_norm[0]), win=win,
        lng=_row(gmlp_ln_g[0]), lnb=_row(gmlp_ln_b[0]),
        lbf=hgrn_lb_fwd.astype(F32), lbb=hgrn_lb_bwd.astype(F32),
        ws=gmlp_w_s[0].astype(BF16), bs=gmlp_b_s[0].T.astype(F32), gon=_row(gmlp_out_norm[0]),
        hon=_row(hgrn_out_norm[0]), wout=wout,
        n2=_row(ffn2_norm[0]), wg2=wg2, wu2=wu2, wd2=wd2, nf=_row(final_norm),
    )
    return _trunk(x_prompt, w), _trunk(x_sample, w)
```
